```python
import jax, jax.numpy as jnp
from jax import lax
import numpy as np

D_MODEL = 2048
BATCH = 2
SEQ = 16384
DEPTH = 2

GRID_W = 64
CTX_LEN = 256
N_MIXERS = 2
N_A = (DEPTH + 1) // 2
N_B = DEPTH // 2
N_MOD = 9
D_FF = 5632
EPS = 1e-6
CHUNK = 128
GM_INNER = D_MODEL
GM_GROUPS = 8
GM_GDIM = GM_INNER // GM_GROUPS
MLA_HEADS = D_MODEL // 128
Q_LORA = 448
KV_LORA = 160
QK_NOPE = 128
QK_ROPE = 64
V_DIM = 128
QK_DIM = QK_NOPE + QK_ROPE
MLA_SCALE = QK_DIM ** -0.5
ROPE_AXIS = QK_ROPE // 2
ROPE_THETA = 10000.0
Q_BLOCK = 128

kernel_name = "hybrid_chunkgmlp_mla_macaron_dit"


def rms_norm(x, g):
    xf = x.astype(jnp.float32)
    y = xf * lax.rsqrt(jnp.mean(xf * xf, axis=-1, keepdims=True) + EPS)
    return y.astype(x.dtype) * g


def layer_norm(x, g, b):
    xf = x.astype(jnp.float32)
    mu = jnp.mean(xf, axis=-1, keepdims=True)
    var = jnp.mean(jnp.square(xf - mu), axis=-1, keepdims=True)
    return ((xf - mu) * lax.rsqrt(var + EPS)).astype(x.dtype) * g + b


def adaln(cond, w, b):
    m = jax.nn.silu(cond) @ w + b
    return m.reshape(m.shape[0], 1, N_MOD, D_MODEL)


def pre_mod(x, m, k, g):
    return rms_norm(x, g) * (1.0 + m[:, :, 3 * k + 1]) + m[:, :, 3 * k]


def swiglu(h, w_gate, w_up, w_down):
    return (jax.nn.silu(h @ w_gate) * (h @ w_up)) @ w_down


def ffn_sublayer(x, m, k, g, w_gate, w_up, w_down):
    h = pre_mod(x, m, k, g)
    return x + 0.5 * m[:, :, 3 * k + 2] * swiglu(h, w_gate, w_up, w_down)


def chunk_gmlp(h, w_in, ln_g, ln_b, w_s, b_s, w_out):
    bsz, n, _ = h.shape
    z = jax.nn.gelu(h @ w_in)
    u, v = jnp.split(z, 2, axis=-1)
    v = layer_norm(v, ln_g, ln_b)
    v = v.reshape(bsz, n // CHUNK, CHUNK, GM_GROUPS, GM_GDIM)
    v = jnp.einsum('gpq,bcqgd->bcpgd', w_s, v) + b_s.T[None, None, :, :, None]
    return (u * v.reshape(bsz, n, GM_INNER)) @ w_out


def axial_rope_tables(n, dtype):
    n_rows = n // GRID_W
    row = jnp.broadcast_to(jnp.arange(n_rows)[:, None], (n_rows, GRID_W)).reshape(n).astype(jnp.float32)
    col = jnp.broadcast_to(jnp.arange(GRID_W)[None, :], (n_rows, GRID_W)).reshape(n).astype(jnp.float32)
    half = ROPE_AXIS // 2
    inv = ROPE_THETA ** (-jnp.arange(half, dtype=jnp.float32) / half)
    ang_r = row[:, None] * inv
    ang_c = col[:, None] * inv
    ang = jnp.concatenate([ang_r, ang_r, ang_c, ang_c], axis=-1)
    return jnp.cos(ang).astype(dtype), jnp.sin(ang).astype(dtype)


def rope_2d(x, cos, sin):
    xs = x.reshape(*x.shape[:-1], 2, 2, ROPE_AXIS // 2)
    rot = jnp.stack([-xs[..., 1, :], xs[..., 0, :]], axis=-2).reshape(x.shape)
    return x * cos[None, :, None, :] + rot * sin[None, :, None, :]


def mla_queries(h, w_dq, q_norm, w_uq, rope):
    bsz, n, _ = h.shape
    q = (rms_norm(h @ w_dq, q_norm) @ w_uq).reshape(bsz, n, MLA_HEADS, QK_DIM)
    q_nope, q_pe = q[..., :QK_NOPE], q[..., QK_NOPE:]
    if rope is not None:
        q_pe = rope_2d(q_pe, *rope)
    return jnp.concatenate([q_nope, q_pe], axis=-1)


def mla_keys_values(h, w_dkv, kv_norm, w_ukv, rope):
    bsz, n, _ = h.shape
    kv = h @ w_dkv
    c_kv = rms_norm(kv[..., :KV_LORA], kv_norm)
    k_pe = kv[..., None, KV_LORA:]
    if rope is not None:
        k_pe = rope_2d(k_pe, *rope)
    kv_up = (c_kv @ w_ukv).reshape(bsz, n, MLA_HEADS, QK_NOPE + V_DIM)
    k_nope, v = kv_up[..., :QK_NOPE], kv_up[..., QK_NOPE:]
    k = jnp.concatenate([k_nope, jnp.broadcast_to(k_pe, (bsz, n, MLA_HEADS, QK_ROPE))], axis=-1)
    return k, v


def softmax_attend(q, k, v):
    s = jnp.einsum('bqhd,bkhd->bhqk', q, k).astype(jnp.float32) * MLA_SCALE
    p = jax.nn.softmax(s, axis=-1).astype(v.dtype)
    return jnp.einsum('bhqk,bkhd->bqhd', p, v)


def mla(hx, hy, w_dq, q_norm, w_uq, w_dkv, kv_norm, w_ukv, w_o, rope, with_ctx_out):
    bsz, n, _ = hx.shape
    ky, vy = mla_keys_values(hy, w_dkv, kv_norm, w_ukv, None)
    kx, vx = mla_keys_values(hx, w_dkv, kv_norm, w_ukv, rope)
    qx = mla_queries(hx, w_dq, q_norm, w_uq, rope)
    k_all = jnp.concatenate([ky, kx], axis=1)
    v_all = jnp.concatenate([vy, vx], axis=1)
    qb = qx.reshape(bsz, n // Q_BLOCK, Q_BLOCK, MLA_HEADS, QK_DIM).swapaxes(0, 1)
    ox = lax.map(lambda q_blk: softmax_attend(q_blk, k_all, v_all), qb)
    ox = ox.swapaxes(0, 1).reshape(bsz, n, MLA_HEADS * V_DIM) @ w_o
    if not with_ctx_out:
        return ox, None
    qy = mla_queries(hy, w_dq, q_norm, w_uq, None)
    oy = softmax_attend(qy, ky, vy).reshape(bsz, hy.shape[1], MLA_HEADS * V_DIM) @ w_o
    return ox, oy


def setup_inputs(seed: int = 0) -> dict:
    key = jax.random.key(seed)
    ks = iter(jax.random.split(key, 32))
    D = D_MODEL

    def nrm(shape, fan_in, mult=1.0):
        return jax.random.normal(next(ks), shape, jnp.float32) * (mult * fan_in ** -0.5)

    def gain(shape):
        return 1.0 + 0.05 * jax.random.normal(next(ks), shape, jnp.float32)

    def bias(shape):
        return 0.02 * jax.random.normal(next(ks), shape, jnp.float32)

    return {
        "x": jax.random.normal(next(ks), (BATCH, SEQ, D), jnp.float32),
        "c": jax.random.normal(next(ks), (BATCH, D), jnp.float32),
        "ctx": jax.random.normal(next(ks), (BATCH, CTX_LEN, D), jnp.float32),
        "c_ctx": jax.random.normal(next(ks), (D,), jnp.float32),
        "mod_w": nrm((DEPTH, D, N_MOD * D), D, 0.5),
        "mod_b": bias((DEPTH, N_MOD * D)),
        "norm_g": gain((DEPTH, 3, D)),
        "ffn_w_gate": nrm((DEPTH, 2, D, D_FF), D),
        "ffn_w_up": nrm((DEPTH, 2, D, D_FF), D),
        "ffn_w_down": nrm((DEPTH, 2, D_FF, D), D_FF),
        "gm_w_in": nrm((N_A, D, 2 * GM_INNER), D),
        "gm_ln_g": gain((N_A, GM_INNER)),
        "gm_ln_b": bias((N_A, GM_INNER)),
        "gm_w_s": nrm((N_A, GM_GROUPS, CHUNK, CHUNK), CHUNK),
        "gm_b_s": bias((N_A, GM_GROUPS, CHUNK)),
        "gm_w_out": nrm((N_A, GM_INNER, D), GM_INNER),
        "mla_w_dq": nrm((N_B, D, Q_LORA), D),
        "mla_q_norm": gain((N_B, Q_LORA)),
        "mla_w_uq": nrm((N_B, Q_LORA, MLA_HEADS * QK_DIM), Q_LORA),
        "mla_w_dkv": nrm((N_B, D, KV_LORA + QK_ROPE), D),
        "mla_kv_norm": gain((N_B, KV_LORA)),
        "mla_w_ukv": nrm((N_B, KV_LORA, MLA_HEADS * (QK_NOPE + V_DIM)), KV_LORA),
        "mla_w_o": nrm((N_B, MLA_HEADS * V_DIM, D), MLA_HEADS * V_DIM),
        "final_g": gain((D,)),
    }


def reference(x, c, ctx, c_ctx, mod_w, mod_b, norm_g, ffn_w_gate, ffn_w_up, ffn_w_down,
              gm_w_in, gm_ln_g, gm_ln_b, gm_w_s, gm_b_s, gm_w_out,
              mla_w_dq, mla_q_norm, mla_w_uq, mla_w_dkv, mla_kv_norm, mla_w_ukv, mla_w_o,
              final_g):
    rope = axial_rope_tables(x.shape[1], x.dtype)
    y = ctx
    for i in range(DEPTH):
        last = i == DEPTH - 1
        use_a = i % N_MIXERS == 0
        j = i // N_MIXERS
        ctx_needed = (not last) or (not use_a)
        m_x = adaln(c, mod_w[i], mod_b[i])
        m_y = adaln(c_ctx[None], mod_w[i], mod_b[i])
        x = ffn_sublayer(x, m_x, 0, norm_g[i, 0], ffn_w_gate[i, 0], ffn_w_up[i, 0], ffn_w_down[i, 0])
        if ctx_needed:
            y = ffn_sublayer(y, m_y, 0, norm_g[i, 0], ffn_w_gate[i, 0], ffn_w_up[i, 0], ffn_w_down[i, 0])
        hx = pre_mod(x, m_x, 1, norm_g[i, 1])
        if use_a:
            gm = (gm_w_in[j], gm_ln_g[j], gm_ln_b[j], gm_w_s[j], gm_b_s[j], gm_w_out[j])
            ox = chunk_gmlp(hx, *gm)
            oy = chunk_gmlp(pre_mod(y, m_y, 1, norm_g[i, 1]), *gm) if not last else None
        else:
            hy = pre_mod(y, m_y, 1, norm_g[i, 1])
            ox, oy = mla(hx, hy, mla_w_dq[j], mla_q_norm[j], mla_w_uq[j], mla_w_dkv[j],
                         mla_kv_norm[j], mla_w_ukv[j], mla_w_o[j], rope, not last)
        x = x + m_x[:, :, 5] * ox
        x = ffn_sublayer(x, m_x, 2, norm_g[i, 2], ffn_w_gate[i, 1], ffn_w_up[i, 1], ffn_w_down[i, 1])
        if not last:
            y = y + m_y[:, :, 5] * oy
            y = ffn_sublayer(y, m_y, 2, norm_g[i, 2], ffn_w_gate[i, 1], ffn_w_up[i, 1], ffn_w_down[i, 1])
    return rms_norm(x, final_g)
```

```python
import functools
import math

import jax
import jax.numpy as jnp
from jax import lax
from jax.experimental import pallas as pl
from jax.experimental.pallas import tpu as pltpu

EPS = 1e-6
N_MOD = 9
GRID_W = 64
CHUNK = 128
GM_GROUPS = 8
QK_NOPE = 128
QK_ROPE = 64
V_DIM = 128
ROPE_THETA = 10000.0
HEAD_W = 256
LANES = 128
VMEM_LIMIT = 56 * 1024 * 1024

BF16 = jnp.bfloat16
F32 = jnp.float32


def _cparams(sem):
    return pltpu.CompilerParams(dimension_semantics=sem, vmem_limit_bytes=VMEM_LIMIT)


def _rms(x, width=None):
    n = x.shape[-1] if width is None else width
    return x * lax.rsqrt(jnp.sum(x * x, axis=-1, keepdims=True) * (1.0 / n) + EPS)


def _premod(x, mod_ref, k, g):
    y = _rms(x) * g
    return y * (1.0 + mod_ref[0, 3 * k + 1:3 * k + 2, :]) + mod_ref[0, 3 * k:3 * k + 1, :]


def _adaln_kernel(c_ref, w_ref, b_ref, o_ref):
    c = c_ref[...]
    s = c * jax.nn.sigmoid(c)
    o_ref[0] = jnp.dot(s, w_ref[0], preferred_element_type=F32,
                       precision=lax.Precision.HIGHEST) + b_ref[0]


def _adaln(cond8, mod_w, mod_b):
    depth, d, n = mod_w.shape
    tn = min(d, 1024)
    assert n % tn == 0
    return pl.pallas_call(
        _adaln_kernel,
        grid=(depth, n // tn),
        in_specs=[
            pl.BlockSpec((8, d), lambda l, j: (0, 0)),
            pl.BlockSpec((1, d, tn), lambda l, j: (l, 0, j)),
            pl.BlockSpec((1, 1, tn), lambda l, j: (l, 0, j)),
        ],
        out_specs=pl.BlockSpec((1, 8, tn), lambda l, j: (l, 0, j)),
        out_shape=jax.ShapeDtypeStruct((depth, 8, n), F32),
        compiler_params=_cparams(("parallel", "parallel")),
        name="adaln",
    )(cond8, mod_w, mod_b.reshape(depth, 1, n))


def _ffn_kernel(*refs, k, final):
    if final:
        x_ref, mod_ref, g_ref, wg_ref, wu_ref, wd_ref, fg_ref, o_ref, h_ref = refs
    else:
        x_ref, mod_ref, g_ref, wg_ref, wu_ref, wd_ref, o_ref, h_ref = refs
    f = pl.program_id(1)

    @pl.when(f == 0)
    def _():
        x = x_ref[...]
        h_ref[...] = _premod(x, mod_ref, k, g_ref[...]).astype(BF16)
        o_ref[...] = x

    h = h_ref[...]
    a = jnp.dot(h, wg_ref[...], preferred_element_type=F32)
    b = jnp.dot(h, wu_ref[...], preferred_element_type=F32)
    p = (a * jax.nn.sigmoid(a) * b).astype(BF16)
    half_gate = 0.5 * mod_ref[0, 3 * k + 2:3 * k + 3, :]
    o_ref[...] += jnp.dot(p, wd_ref[...], preferred_element_type=F32) * half_gate

    if final:
        @pl.when(f == pl.num_programs(1) - 1)
        def _():
            o_ref[...] = _rms(o_ref[...]) * fg_ref[...]


def _ffn(x, mod, k, g, wg, wu, wd, final_g=None):
    t, d = x.shape
    nb = mod.shape[0]
    ff = wg.shape[1]
    tm = min(512, t // nb)
    tf = min(512, ff)
    tiles_per_mod = (t // nb) // tm
    final = final_g is not None
    in_specs = [
        pl.BlockSpec((tm, d), lambda i, f: (i, 0)),
        pl.BlockSpec((1, N_MOD, d), lambda i, f: (i // tiles_per_mod, 0, 0)),
        pl.BlockSpec((1, d), lambda i, f: (0, 0)),
        pl.BlockSpec((d, tf), lambda i, f: (0, f)),
        pl.BlockSpec((d, tf), lambda i, f: (0, f)),
        pl.BlockSpec((tf, d), lambda i, f: (f, 0)),
    ]
    args = [x, mod, g.reshape(1, d), wg, wu, wd]
    if final:
        in_specs.append(pl.BlockSpec((1, d), lambda i, f: (0, 0)))
        args.append(final_g.reshape(1, d))
    return pl.pallas_call(
        functools.partial(_ffn_kernel, k=k, final=final),
        grid=(t // tm, ff // tf),
        in_specs=in_specs,
        out_specs=pl.BlockSpec((tm, d), lambda i, f: (i, 0)),
        out_shape=jax.ShapeDtypeStruct((t, d), F32),
        scratch_shapes=[pltpu.VMEM((tm, d), BF16)],
        compiler_params=_cparams(("parallel", "arbitrary")),
        name="ffn",
    )(*args)


def _proj_res_kernel(x_ref, a_ref, w_ref, mod_ref, o_ref):
    y = jnp.dot(a_ref[...], w_ref[...], preferred_element_type=F32)
    o_ref[...] = x_ref[...] + mod_ref[0, 5:6, :] * y


def _proj_res(x, a, w, mod):
    t, d = x.shape
    nb = mod.shape[0]
    din = a.shape[1]
    tm = min(512, t // nb)
    tiles_per_mod = (t // nb) // tm
    return pl.pallas_call(
        _proj_res_kernel,
        grid=(t // tm,),
        in_specs=[
            pl.BlockSpec((tm, d), lambda i: (i, 0)),
            pl.BlockSpec((tm, din), lambda i: (i, 0)),
            pl.BlockSpec((din, d), lambda i: (0, 0)),
            pl.BlockSpec((1, N_MOD, d), lambda i: (i // tiles_per_mod, 0, 0)),
        ],
        out_specs=pl.BlockSpec((tm, d), lambda i: (i, 0)),
        out_shape=jax.ShapeDtypeStruct((t, d), F32),
        compiler_params=_cparams(("parallel",)),
        name="proj_res",
    )(x, a, w, mod)


def _gmlp_kernel(x_ref, mod_ref, g_ref, wu_ref, wv_ref, lng_ref, lnb_ref, ws_ref,
                 bs_ref, o_ref, v_ref):
    tm, inner = o_ref.shape
    gdim = inner // GM_GROUPS
    h = _premod(x_ref[...], mod_ref, 1, g_ref[...]).astype(BF16)
    v = jax.nn.gelu(jnp.dot(h, wv_ref[...], preferred_element_type=F32))
    mu = jnp.mean(v, axis=-1, keepdims=True)
    vc = v - mu
    var = jnp.mean(vc * vc, axis=-1, keepdims=True)
    v_ref[...] = ((vc * lax.rsqrt(var + EPS)) * lng_ref[...] + lnb_ref[...]).astype(BF16)
    for c in range(tm // CHUNK):
        rows = slice(c * CHUNK, (c + 1) * CHUNK)
        for gi in range(GM_GROUPS):
            cols = slice(gi * gdim, (gi + 1) * gdim)
            mixed = jnp.dot(ws_ref[gi], v_ref[rows, cols], preferred_element_type=F32)
            o_ref[rows, cols] = (mixed + bs_ref[:, cols]).astype(BF16)
    u = jax.nn.gelu(jnp.dot(h, wu_ref[...], preferred_element_type=F32))
    o_ref[...] = (u * o_ref[...].astype(F32)).astype(BF16)


def _gmlp_gate(x, mod, g, w_u, w_v, ln_g, ln_b, w_s, b_s_full):
    t, d = x.shape
    nb = mod.shape[0]
    inner = w_u.shape[1]
    tm = min(256, t // nb)
    tiles_per_mod = (t // nb) // tm
    const2 = lambda i: (0, 0)
    return pl.pallas_call(
        _gmlp_kernel,
        grid=(t // tm,),
        in_specs=[
            pl.BlockSpec((tm, d), lambda i: (i, 0)),
            pl.BlockSpec((1, N_MOD, d), lambda i: (i // tiles_per_mod, 0, 0)),
            pl.BlockSpec((1, d), const2),
            pl.BlockSpec((d, inner), const2),
            pl.BlockSpec((d, inner), const2),
            pl.BlockSpec((1, inner), const2),
            pl.BlockSpec((1, inner), const2),
            pl.BlockSpec((GM_GROUPS, CHUNK, CHUNK), lambda i: (0, 0, 0)),
            pl.BlockSpec((CHUNK, inner), const2),
        ],
        out_specs=pl.BlockSpec((tm, inner), lambda i: (i, 0)),
        out_shape=jax.ShapeDtypeStruct((t, inner), BF16),
        scratch_shapes=[pltpu.VMEM((tm, inner), BF16)],
        compiler_params=_cparams(("parallel",)),
        name="gmlp_gate",
    )(x, mod, g.reshape(1, d), w_u, w_v, ln_g.reshape(1, inner), ln_b.reshape(1, inner),
      w_s, b_s_full)


def _mla_proj_kernel(*refs, heads, q_lora, kv_lora, q_scale, with_q):
    if with_q:
        (x_ref, mod_ref, g_ref, cos_ref, sin_ref, wdq_ref, qn_ref, wq_ref, wqr_ref,
         wkv_ref, kvn_ref, wukv_ref, q_ref, k_ref, v_ref) = refs
    else:
        (x_ref, mod_ref, g_ref, cos_ref, sin_ref,
         wkv_ref, kvn_ref, wukv_ref, k_ref, v_ref) = refs
    h = _premod(x_ref[0], mod_ref, 1, g_ref[...]).astype(BF16)
    cos = cos_ref[...]
    sin = sin_ref[...]

    kv = jnp.dot(h, wkv_ref[...], preferred_element_type=F32)
    c_kv = (_rms(kv[:, :2 * LANES], kv_lora) * kvn_ref[...]).astype(BF16)
    k_pe = (kv[:, 2 * LANES:3 * LANES] * cos + kv[:, 3 * LANES:] * sin).astype(BF16)
    up = jnp.dot(c_kv, wukv_ref[...], preferred_element_type=F32)
    nope_w = heads * QK_NOPE
    v_ref[0] = up[:, nope_w:].astype(BF16)
    for hd in range(heads):
        k_ref[0, :, hd * HEAD_W:hd * HEAD_W + LANES] = k_pe
        k_ref[0, :, hd * HEAD_W + LANES:(hd + 1) * HEAD_W] = (
            up[:, hd * QK_NOPE:(hd + 1) * QK_NOPE].astype(BF16))

    if with_q:
        ql = jnp.dot(h, wdq_ref[...], preferred_element_type=F32)
        qn = (_rms(ql, q_lora) * qn_ref[...] * q_scale).astype(BF16)
        qa = jnp.dot(qn, wq_ref[...], preferred_element_type=F32)
        qr = jnp.dot(qn, wqr_ref[...], preferred_element_type=F32)
        for hd in range(heads):
            pe = qa[:, hd * HEAD_W:hd * HEAD_W + LANES] * cos + qr[:, hd * LANES:(hd + 1) * LANES] * sin
            q_ref[0, :, hd * HEAD_W:hd * HEAD_W + LANES] = pe.astype(BF16)
            q_ref[0, :, hd * HEAD_W + LANES:(hd + 1) * HEAD_W] = (
                qa[:, hd * HEAD_W + LANES:(hd + 1) * HEAD_W].astype(BF16))


def _mla_proj(x3, mod, g, cos, sin, wts, *, heads, q_lora, kv_lora, q_scale, with_q):
    bsz, n, d = x3.shape
    nb = mod.shape[0]
    tm = min(256, n)
    nt = n // tm
    mod_idx = (lambda b, i: (b, 0, 0)) if nb == bsz else (lambda b, i: (0, 0, 0))
    const2 = lambda b, i: (0, 0)
    w_dq, q_norm, w_q, w_qr, w_kv, kv_norm, w_ukv = wts
    in_specs = [
        pl.BlockSpec((1, tm, d), lambda b, i: (b, i, 0)),
        pl.BlockSpec((1, N_MOD, d), mod_idx),
        pl.BlockSpec((1, d), const2),
        pl.BlockSpec((tm, LANES), lambda b, i: (i, 0)),
        pl.BlockSpec((tm, LANES), lambda b, i: (i, 0)),
    ]
    args = [x3, mod, g.reshape(1, d), cos, sin]
    if with_q:
        in_specs += [pl.BlockSpec(w_dq.shape, const2), pl.BlockSpec(q_norm.shape, const2),
                     pl.BlockSpec(w_q.shape, const2), pl.BlockSpec(w_qr.shape, const2)]
        args += [w_dq, q_norm, w_q, w_qr]
    in_specs += [pl.BlockSpec(w_kv.shape, const2), pl.BlockSpec(kv_norm.shape, const2),
                 pl.BlockSpec(w_ukv.shape, const2)]
    args += [w_kv, kv_norm, w_ukv]
    out_specs = [pl.BlockSpec((1, tm, heads * HEAD_W), lambda b, i: (b, i, 0)),
                 pl.BlockSpec((1, tm, heads * V_DIM), lambda b, i: (b, i, 0))]
    out_shape = [jax.ShapeDtypeStruct((bsz, n, heads * HEAD_W), BF16),
                 jax.ShapeDtypeStruct((bsz, n, heads * V_DIM), BF16)]
    if with_q:
        out_specs = [out_specs[0]] + out_specs
        out_shape = [out_shape[0]] + out_shape
    return pl.pallas_call(
        functools.partial(_mla_proj_kernel, heads=heads, q_lora=q_lora, kv_lora=kv_lora,
                          q_scale=q_scale, with_q=with_q),
        grid=(bsz, nt),
        in_specs=in_specs,
        out_specs=out_specs,
        out_shape=out_shape,
        compiler_params=_cparams(("parallel", "parallel")),
        name="mla_proj_q" if with_q else "mla_proj_kv",
    )(*args)


def _nt_dot(a, b):
    return lax.dot_general(a, b, (((1,), (1,)), ((), ())), preferred_element_type=F32)


def _flash_kernel(q_ref, kc_ref, vc_ref, kx_ref, vx_ref, o_ref, *, tk):
    q = q_ref[0]
    n_keys = kx_ref.shape[1]

    s = _nt_dot(q, kc_ref[0])
    m0 = jnp.max(s, axis=-1, keepdims=True)
    p = jnp.exp2(s - m0)
    l0 = jnp.sum(p, axis=-1, keepdims=True)
    acc0 = jnp.dot(p.astype(BF16), vc_ref[0], preferred_element_type=F32)

    def body(j, carry):
        m, l, acc = carry
        start = pl.multiple_of(j * tk, tk)
        s = _nt_dot(q, kx_ref[0, pl.ds(start, tk), :])
        m_new = jnp.maximum(m, jnp.max(s, axis=-1, keepdims=True))
        alpha = jnp.exp2(m - m_new)
        p = jnp.exp2(s - m_new)
        l = alpha * l + jnp.sum(p, axis=-1, keepdims=True)
        acc = alpha * acc + jnp.dot(p.astype(BF16), vx_ref[0, pl.ds(start, tk), :],
                                    preferred_element_type=F32)
        return m_new, l, acc

    _, l, acc = lax.fori_loop(0, n_keys // tk, body, (m0, l0, acc0))
    o_ref[0] = (acc / l).astype(BF16)


def _flash(q, kc, vc, kx, vx, *, heads):
    bsz, n, _ = q.shape
    nc = kc.shape[1]
    tq = min(512, n)
    tk = min(1024, n)
    return pl.pallas_call(
        functools.partial(_flash_kernel, tk=tk),
        grid=(bsz, heads, n // tq),
        in_specs=[
            pl.BlockSpec((1, tq, HEAD_W), lambda b, h, i: (b, i, h)),
            pl.BlockSpec((1, nc, HEAD_W), lambda b, h, i: (b, 0, h)),
            pl.BlockSpec((1, nc, V_DIM), lambda b, h, i: (b, 0, h)),
            pl.BlockSpec((1, n, HEAD_W), lambda b, h, i: (b, 0, h)),
            pl.BlockSpec((1, n, V_DIM), lambda b, h, i: (b, 0, h)),
        ],
        out_specs=pl.BlockSpec((1, tq, V_DIM), lambda b, h, i: (b, i, h)),
        out_shape=jax.ShapeDtypeStruct((bsz, n, heads * V_DIM), BF16),
        compiler_params=_cparams(("parallel", "parallel", "arbitrary")),
        name="flash",
    )(q, kc, vc, kx, vx)


def _rope_rot_cols(w):
    q = QK_ROPE // 4
    parts = w.reshape(*w.shape[:-1], 2, 2, q)
    rot = jnp.stack([-parts[..., 1, :], parts[..., 0, :]], axis=-2)
    return rot.reshape(w.shape)


def _pad_to(a, size, axis):
    pad = [(0, 0)] * a.ndim
    pad[axis] = (0, size - a.shape[axis])
    return jnp.pad(a, pad)


def _mla_weights(w_dq, q_norm, w_uq, w_dkv, kv_norm, w_ukv, heads):
    d, q_lora = w_dq.shape
    kv_lora = kv_norm.shape[0]
    qk_dim = QK_NOPE + QK_ROPE
    ql_pad = -(-q_lora // LANES) * LANES
    w_dq_p = _pad_to(w_dq, ql_pad, 1).astype(BF16)
    q_norm_p = _pad_to(q_norm, ql_pad, 0).reshape(1, ql_pad)
    uq = w_uq.reshape(q_lora, heads, qk_dim)
    uq_nope, uq_pe = uq[..., :QK_NOPE], uq[..., QK_NOPE:]
    zeros = jnp.zeros((q_lora, heads, LANES - QK_ROPE), w_uq.dtype)
    w_q = jnp.concatenate([uq_pe, zeros, uq_nope], axis=-1).reshape(q_lora, heads * HEAD_W)
    w_qr = jnp.concatenate([_rope_rot_cols(uq_pe), zeros], axis=-1).reshape(q_lora, heads * LANES)
    w_q = _pad_to(w_q, ql_pad, 0).astype(BF16)
    w_qr = _pad_to(w_qr, ql_pad, 0).astype(BF16)
    w_c = _pad_to(w_dkv[:, :kv_lora], 2 * LANES, 1)
    w_pe = w_dkv[:, kv_lora:]
    w_kv = jnp.concatenate([w_c, _pad_to(w_pe, LANES, 1), _pad_to(_rope_rot_cols(w_pe), LANES, 1)],
                           axis=1).astype(BF16)
    kv_norm_p = _pad_to(kv_norm, 2 * LANES, 0).reshape(1, 2 * LANES)
    ukv = w_ukv.reshape(kv_lora, heads, QK_NOPE + V_DIM)
    w_ukv_p = jnp.concatenate([ukv[..., :QK_NOPE].reshape(kv_lora, heads * QK_NOPE),
                               ukv[..., QK_NOPE:].reshape(kv_lora, heads * V_DIM)], axis=1)
    w_ukv_p = _pad_to(w_ukv_p, 2 * LANES, 0).astype(BF16)
    return (w_dq_p, q_norm_p, w_q, w_qr, w_kv, kv_norm_p, w_ukv_p), q_lora, kv_lora


def _rope_tables(n):
    t = jnp.arange(n)
    row = (t // GRID_W).astype(F32)
    col = (t % GRID_W).astype(F32)
    half = QK_ROPE // 4
    inv = ROPE_THETA ** (-jnp.arange(half, dtype=F32) / half)
    ang_r = row[:, None] * inv
    ang_c = col[:, None] * inv
    ang = jnp.concatenate([ang_r, ang_r, ang_c, ang_c], axis=-1)
    cos = _pad_to(jnp.cos(ang), LANES, 1)
    sin = _pad_to(jnp.sin(ang), LANES, 1)
    return cos, sin


def kernel(x, c, ctx, c_ctx, mod_w, mod_b, norm_g, ffn_w_gate, ffn_w_up, ffn_w_down,
           gm_w_in, gm_ln_g, gm_ln_b, gm_w_s, gm_b_s, gm_w_out,
           mla_w_dq, mla_q_norm, mla_w_uq, mla_w_dkv, mla_kv_norm, mla_w_ukv, mla_w_o,
           final_g):
    bsz, n, d = x.shape
    n_ctx = ctx.shape[1]
    depth = mod_w.shape[0]
    heads = d // V_DIM
    assert bsz + 1 <= 8

    cond8 = jnp.concatenate([c, c_ctx[None], jnp.zeros((8 - bsz - 1, d), F32)], axis=0)
    mods = _adaln(cond8, mod_w, mod_b).reshape(depth, 8, N_MOD, d)

    xt = x.reshape(bsz * n, d)
    yt = ctx.reshape(bsz * n_ctx, d)
    wg = ffn_w_gate.astype(BF16)
    wu = ffn_w_up.astype(BF16)
    wd = ffn_w_down.astype(BF16)

    for i in range(depth):
        last = i == depth - 1
        use_a = i % 2 == 0
        j = i // 2
        m_x = mods[i, :bsz]
        m_y = mods[i, bsz:bsz + 1]
        ctx_needed = (not last) or (not use_a)

        xt = _ffn(xt, m_x, 0, norm_g[i, 0], wg[i, 0], wu[i, 0], wd[i, 0])
        if ctx_needed:
            yt = _ffn(yt, m_y, 0, norm_g[i, 0], wg[i, 0], wu[i, 0], wd[i, 0])

        if use_a:
            inner = gm_w_out.shape[1]
            w_in = gm_w_in[j].astype(BF16)
            w_u, w_v = w_in[:, :inner], w_in[:, inner:]
            w_s = gm_w_s[j].astype(BF16)
            b_full = jnp.repeat(gm_b_s[j].T, inner // GM_GROUPS, axis=1)
            w_out = gm_w_out[j].astype(BF16)
            gm = (norm_g[i, 1], w_u, w_v, gm_ln_g[j], gm_ln_b[j], w_s, b_full)
            xt = _proj_res(xt, _gmlp_gate(xt, m_x, *gm), w_out, m_x)
            if not last:
                yt = _proj_res(yt, _gmlp_gate(yt, m_y, *gm), w_out, m_y)
        else:
            wts, q_lora, kv_lora = _mla_weights(mla_w_dq[j], mla_q_norm[j], mla_w_uq[j],
                                                mla_w_dkv[j], mla_kv_norm[j], mla_w_ukv[j], heads)
            q_scale = (QK_NOPE + QK_ROPE) ** -0.5 * math.log2(math.e)
            cos, sin = _rope_tables(n)
            ones = _pad_to(jnp.ones((n_ctx, QK_ROPE), F32), LANES, 1)
            zeros = jnp.zeros((n_ctx, LANES), F32)
            kw = dict(heads=heads, q_lora=q_lora, kv_lora=kv_lora, q_scale=q_scale)
            kc, vc = _mla_proj(yt.reshape(bsz, n_ctx, d), m_y, norm_g[i, 1], ones, zeros, wts,
                               with_q=False, **kw)
            q, kx, vx = _mla_proj(xt.reshape(bsz, n, d), m_x, norm_g[i, 1], cos, sin, wts,
                                  with_q=True, **kw)
            att = _flash(q, kc, vc, kx, vx, heads=heads)
            xt = _proj_res(xt, att.reshape(bsz * n, heads * V_DIM), mla_w_o[j].astype(BF16), m_x)
            assert last, "context attention output is only needed for non-final MLA layers"

        xt = _ffn(xt, m_x, 2, norm_g[i, 2], wg[i, 1], wu[i, 1], wd[i, 1],
                  final_g=final_g if last else None)
        if not last:
            yt = _ffn(yt, m_y, 2, norm_g[i, 2], wg[i, 1], wu[i, 1], wd[i, 1])

    return xt.reshape(bsz, n, d)
```

```python
import functools
import math

import jax
import jax.numpy as jnp
from jax import lax
from jax.experimental import pallas as pl
from jax.experimental.pallas import tpu as pltpu

EPS = 1e-6
N_MOD = 9
GRID_W = 64
CHUNK = 128
GM_GROUPS = 8
QK_NOPE = 128
QK_ROPE = 64
V_DIM = 128
ROPE_THETA = 10000.0
HEAD_W = 256
LANES = 128
VMEM_LIMIT = 56 * 1024 * 1024
VMEM_LIMIT_FFN = 60 * 1024 * 1024

BF16 = jnp.bfloat16
F32 = jnp.float32


def _cparams(sem, vmem=VMEM_LIMIT):
    return pltpu.CompilerParams(dimension_semantics=sem, vmem_limit_bytes=vmem)


def _rms(x, width=None):
    n = x.shape[-1] if width is None else width
    return x * lax.rsqrt(jnp.sum(x * x, axis=-1, keepdims=True) * (1.0 / n) + EPS)


def _premod(x, mod_ref, k, g):
    y = _rms(x) * g
    return y * (1.0 + mod_ref[0, 3 * k + 1:3 * k + 2, :]) + mod_ref[0, 3 * k:3 * k + 1, :]


def _adaln_kernel(c_ref, w_ref, b_ref, o_ref):
    c = c_ref[...]
    s = c * jax.nn.sigmoid(c)
    o_ref[0] = jnp.dot(s, w_ref[0], preferred_element_type=F32,
                       precision=lax.Precision.HIGHEST) + b_ref[0]


def _adaln(cond8, mod_w, mod_b):
    depth, d, n = mod_w.shape
    tn = min(d, 1024)
    assert n % tn == 0
    return pl.pallas_call(
        _adaln_kernel,
        grid=(depth, n // tn),
        in_specs=[
            pl.BlockSpec((8, d), lambda l, j: (0, 0)),
            pl.BlockSpec((1, d, tn), lambda l, j: (l, 0, j)),
            pl.BlockSpec((1, 1, tn), lambda l, j: (l, 0, j)),
        ],
        out_specs=pl.BlockSpec((1, 8, tn), lambda l, j: (l, 0, j)),
        out_shape=jax.ShapeDtypeStruct((depth, 8, n), F32),
        compiler_params=_cparams(("parallel", "parallel")),
        name="adaln",
    )(cond8, mod_w, mod_b.reshape(depth, 1, n))


def _ffn_kernel(*refs, k, final):
    if final:
        x_ref, mod_ref, g_ref, wg_ref, wu_ref, wd_ref, fg_ref, o_ref, h_ref = refs
    else:
        x_ref, mod_ref, g_ref, wg_ref, wu_ref, wd_ref, o_ref, h_ref = refs
    f = pl.program_id(1)

    @pl.when(f == 0)
    def _():
        x = x_ref[...]
        h_ref[...] = _premod(x, mod_ref, k, g_ref[...]).astype(BF16)
        o_ref[...] = x

    h = h_ref[...]
    a = jnp.dot(h, wg_ref[...], preferred_element_type=F32)
    b = jnp.dot(h, wu_ref[...], preferred_element_type=F32)
    p = (a * jax.nn.sigmoid(a) * b).astype(BF16)
    half_gate = 0.5 * mod_ref[0, 3 * k + 2:3 * k + 3, :]
    o_ref[...] += jnp.dot(p, wd_ref[...], preferred_element_type=F32) * half_gate

    if final:
        @pl.when(f == pl.num_programs(1) - 1)
        def _():
            o_ref[...] = _rms(o_ref[...]) * fg_ref[...]


def _ffn(x, mod, k, g, wg, wu, wd, final_g=None):
    t, d = x.shape
    nb = mod.shape[0]
    ff = wg.shape[1]
    tm = min(1024, t // nb)
    tf = min(256 if tm * d >= 1024 * 2048 else 512, ff)
    tiles_per_mod = (t // nb) // tm
    final = final_g is not None
    in_specs = [
        pl.BlockSpec((tm, d), lambda i, f: (i, 0)),
        pl.BlockSpec((1, N_MOD, d), lambda i, f: (i // tiles_per_mod, 0, 0)),
        pl.BlockSpec((1, d), lambda i, f: (0, 0)),
        pl.BlockSpec((d, tf), lambda i, f: (0, f)),
        pl.BlockSpec((d, tf), lambda i, f: (0, f)),
        pl.BlockSpec((tf, d), lambda i, f: (f, 0)),
    ]
    args = [x, mod, g.reshape(1, d), wg, wu, wd]
    if final:
        in_specs.append(pl.BlockSpec((1, d), lambda i, f: (0, 0)))
        args.append(final_g.reshape(1, d))
    return pl.pallas_call(
        functools.partial(_ffn_kernel, k=k, final=final),
        grid=(t // tm, ff // tf),
        in_specs=in_specs,
        out_specs=pl.BlockSpec((tm, d), lambda i, f: (i, 0)),
        out_shape=jax.ShapeDtypeStruct((t, d), F32),
        scratch_shapes=[pltpu.VMEM((tm, d), BF16)],
        compiler_params=_cparams(("parallel", "arbitrary"), VMEM_LIMIT_FFN),
        name="ffn",
    )(*args)


def _proj_res_kernel(x_ref, a_ref, w_ref, mod_ref, o_ref):
    y = jnp.dot(a_ref[...], w_ref[...], preferred_element_type=F32)
    o_ref[...] = x_ref[...] + mod_ref[0, 5:6, :] * y


def _proj_res(x, a, w, mod):
    t, d = x.shape
    nb = mod.shape[0]
    din = a.shape[1]
    tm = min(512, t // nb)
    tiles_per_mod = (t // nb) // tm
    return pl.pallas_call(
        _proj_res_kernel,
        grid=(t // tm,),
        in_specs=[
            pl.BlockSpec((tm, d), lambda i: (i, 0)),
            pl.BlockSpec((tm, din), lambda i: (i, 0)),
            pl.BlockSpec((din, d), lambda i: (0, 0)),
            pl.BlockSpec((1, N_MOD, d), lambda i: (i // tiles_per_mod, 0, 0)),
        ],
        out_specs=pl.BlockSpec((tm, d), lambda i: (i, 0)),
        out_shape=jax.ShapeDtypeStruct((t, d), F32),
        compiler_params=_cparams(("parallel",)),
        name="proj_res",
    )(x, a, w, mod)


def _gmlp_kernel(x_ref, mod_ref, g_ref, wu_ref, wv_ref, lng_ref, lnb_ref, ws_ref,
                 bs_ref, o_ref, v_ref):
    tm, inner = o_ref.shape
    gdim = inner // GM_GROUPS
    h = _premod(x_ref[...], mod_ref, 1, g_ref[...]).astype(BF16)
    v = jax.nn.gelu(jnp.dot(h, wv_ref[...], preferred_element_type=F32))
    mu = jnp.mean(v, axis=-1, keepdims=True)
    vc = v - mu
    var = jnp.mean(vc * vc, axis=-1, keepdims=True)
    v_ref[...] = ((vc * lax.rsqrt(var + EPS)) * lng_ref[...] + lnb_ref[...]).astype(BF16)
    for c in range(tm // CHUNK):
        rows = slice(c * CHUNK, (c + 1) * CHUNK)
        for gi in range(GM_GROUPS):
            cols = slice(gi * gdim, (gi + 1) * gdim)
            mixed = jnp.dot(ws_ref[gi], v_ref[rows, cols], preferred_element_type=F32)
            o_ref[rows, cols] = (mixed + bs_ref[:, cols]).astype(BF16)
    u = jax.nn.gelu(jnp.dot(h, wu_ref[...], preferred_element_type=F32))
    o_ref[...] = (u * o_ref[...].astype(F32)).astype(BF16)


def _gmlp_gate(x, mod, g, w_u, w_v, ln_g, ln_b, w_s, b_s_full):
    t, d = x.shape
    nb = mod.shape[0]
    inner = w_u.shape[1]
    tm = min(256, t // nb)
    tiles_per_mod = (t // nb) // tm
    const2 = lambda i: (0, 0)
    return pl.pallas_call(
        _gmlp_kernel,
        grid=(t // tm,),
        in_specs=[
            pl.BlockSpec((tm, d), lambda i: (i, 0)),
            pl.BlockSpec((1, N_MOD, d), lambda i: (i // tiles_per_mod, 0, 0)),
            pl.BlockSpec((1, d), const2),
            pl.BlockSpec((d, inner), const2),
            pl.BlockSpec((d, inner), const2),
            pl.BlockSpec((1, inner), const2),
            pl.BlockSpec((1, inner), const2),
            pl.BlockSpec((GM_GROUPS, CHUNK, CHUNK), lambda i: (0, 0, 0)),
            pl.BlockSpec((CHUNK, inner), const2),
        ],
        out_specs=pl.BlockSpec((tm, inner), lambda i: (i, 0)),
        out_shape=jax.ShapeDtypeStruct((t, inner), BF16),
        scratch_shapes=[pltpu.VMEM((tm, inner), BF16)],
        compiler_params=_cparams(("parallel",)),
        name="gmlp_gate",
    )(x, mod, g.reshape(1, d), w_u, w_v, ln_g.reshape(1, inner), ln_b.reshape(1, inner),
      w_s, b_s_full)


def _mla_proj_kernel(*refs, heads, q_lora, kv_lora, q_scale, with_q):
    if with_q:
        (x_ref, mod_ref, g_ref, cos_ref, sin_ref, cost_ref, sint_ref, wdq_ref, qn_ref, wqt_ref,
         wqrt_ref, wkv_ref, kvn_ref, wuk_ref, wuvt_ref, qt_ref, k_ref, vt_ref) = refs
    else:
        (x_ref, mod_ref, g_ref, cos_ref, sin_ref,
         wkv_ref, kvn_ref, wuk_ref, wuvt_ref, k_ref, vt_ref) = refs
    h = _premod(x_ref[0], mod_ref, 1, g_ref[...]).astype(BF16)

    kv = jnp.dot(h, wkv_ref[...], preferred_element_type=F32)
    c_kv = _rms(kv[:, :2 * LANES], kv_lora) * kvn_ref[...]
    k_pe = (kv[:, 2 * LANES:3 * LANES] * cos_ref[...] + kv[:, 3 * LANES:] * sin_ref[...]).astype(BF16)
    up_k = jnp.dot(c_kv.astype(BF16), wuk_ref[...], preferred_element_type=F32)
    for hd in range(heads):
        k_ref[0, :, hd * HEAD_W:hd * HEAD_W + LANES] = k_pe
        k_ref[0, :, hd * HEAD_W + LANES:(hd + 1) * HEAD_W] = (
            up_k[:, hd * QK_NOPE:(hd + 1) * QK_NOPE].astype(BF16))
    vt_ref[0] = jnp.dot(wuvt_ref[...], c_kv.T.astype(BF16), preferred_element_type=F32).astype(BF16)

    if with_q:
        ql = jnp.dot(h, wdq_ref[...], preferred_element_type=F32)
        qn_t = (_rms(ql, q_lora) * qn_ref[...] * q_scale).T.astype(BF16)
        qa_t = jnp.dot(wqt_ref[...], qn_t, preferred_element_type=F32)
        qr_t = jnp.dot(wqrt_ref[...], qn_t, preferred_element_type=F32)
        cos_t = cost_ref[...]
        sin_t = sint_ref[...]
        for hd in range(heads):
            pe = (qa_t[hd * HEAD_W:hd * HEAD_W + LANES] * cos_t
                  + qr_t[hd * LANES:(hd + 1) * LANES] * sin_t)
            qt_ref[0, hd * HEAD_W:hd * HEAD_W + LANES, :] = pe.astype(BF16)
            qt_ref[0, hd * HEAD_W + LANES:(hd + 1) * HEAD_W, :] = (
                qa_t[hd * HEAD_W + LANES:(hd + 1) * HEAD_W].astype(BF16))


def _mla_proj(x3, mod, g, cos, sin, wts, *, heads, q_lora, kv_lora, q_scale, with_q):
    bsz, n, d = x3.shape
    nb = mod.shape[0]
    tm = min(256, n)
    mod_idx = (lambda b, i: (b, 0, 0)) if nb == bsz else (lambda b, i: (0, 0, 0))
    const2 = lambda b, i: (0, 0)
    w_dq, q_norm, w_qt, w_qrt, w_kv, kv_norm, w_uk, w_uvt = wts
    in_specs = [
        pl.BlockSpec((1, tm, d), lambda b, i: (b, i, 0)),
        pl.BlockSpec((1, N_MOD, d), mod_idx),
        pl.BlockSpec((1, d), const2),
        pl.BlockSpec((tm, LANES), lambda b, i: (i, 0)),
        pl.BlockSpec((tm, LANES), lambda b, i: (i, 0)),
    ]
    args = [x3, mod, g.reshape(1, d), cos, sin]
    if with_q:
        in_specs += [pl.BlockSpec((LANES, tm), lambda b, i: (0, i)),
                     pl.BlockSpec((LANES, tm), lambda b, i: (0, i))]
        args += [cos.T, sin.T]
        in_specs += [pl.BlockSpec(w.shape, const2) for w in (w_dq, q_norm, w_qt, w_qrt)]
        args += [w_dq, q_norm, w_qt, w_qrt]
    in_specs += [pl.BlockSpec(w.shape, const2) for w in (w_kv, kv_norm, w_uk, w_uvt)]
    args += [w_kv, kv_norm, w_uk, w_uvt]
    out_specs = [pl.BlockSpec((1, tm, heads * HEAD_W), lambda b, i: (b, i, 0)),
                 pl.BlockSpec((1, heads * V_DIM, tm), lambda b, i: (b, 0, i))]
    out_shape = [jax.ShapeDtypeStruct((bsz, n, heads * HEAD_W), BF16),
                 jax.ShapeDtypeStruct((bsz, heads * V_DIM, n), BF16)]
    if with_q:
        out_specs = [pl.BlockSpec((1, heads * HEAD_W, tm), lambda b, i: (b, 0, i))] + out_specs
        out_shape = [jax.ShapeDtypeStruct((bsz, heads * HEAD_W, n), BF16)] + out_shape
    return pl.pallas_call(
        functools.partial(_mla_proj_kernel, heads=heads, q_lora=q_lora, kv_lora=kv_lora,
                          q_scale=q_scale, with_q=with_q),
        grid=(bsz, n // tm),
        in_specs=in_specs,
        out_specs=out_specs,
        out_shape=out_shape,
        compiler_params=_cparams(("parallel", "parallel")),
        name="mla_proj_q" if with_q else "mla_proj_kv",
    )(*args)


def _col_max(s):
    return jnp.max(s, axis=0, keepdims=True)


def _col_sum(p):
    return jnp.sum(p, axis=0, keepdims=True)


def _flash_kernel(qt_ref, kc_ref, vct_ref, kx_ref, vxt_ref, o_ref,
                  s_a, s_b, p_a, p_b, acc_ref, *, tk, unroll):
    qt = qt_ref[0]
    n_chunks = kx_ref.shape[1] // tk

    s = jnp.dot(kc_ref[0], qt, preferred_element_type=F32)
    m = _col_max(s)
    p = jnp.exp2(s - m)
    l = _col_sum(p)
    acc_ref[...] = jnp.dot(vct_ref[0], p.astype(BF16), preferred_element_type=F32)
    s = jnp.dot(kx_ref[0, :tk, :], qt, preferred_element_type=F32)
    s_a[...] = s
    p_b[...] = jnp.zeros_like(p_b)

    def step(j, s_in, s_out, p_in, p_out, carry):
        m, l, alpha_prev, cmax = carry
        nxt = pl.multiple_of(jnp.minimum(j + 1, n_chunks - 1) * tk, tk)
        prv = pl.multiple_of(jnp.maximum(j - 1, 0) * tk, tk)
        s_next = jnp.dot(kx_ref[0, pl.ds(nxt, tk), :], qt, preferred_element_type=F32)
        s_out[...] = s_next
        cmax_next = _col_max(s_next)
        acc_ref[...] = alpha_prev * acc_ref[...] + jnp.dot(
            vxt_ref[0, :, pl.ds(prv, tk)], p_in[...], preferred_element_type=F32)
        m_new = jnp.maximum(m, cmax)
        alpha = jnp.exp2(m - m_new)
        p = jnp.exp2(s_in[...] - m_new)
        p_out[...] = p.astype(BF16)
        return m_new, alpha * l + _col_sum(p), alpha, cmax_next

    def body(i, carry):
        for u in range(0, unroll, 2):
            carry = step(unroll * i + u, s_a, s_b, p_b, p_a, carry)
            carry = step(unroll * i + u + 1, s_b, s_a, p_a, p_b, carry)
        return carry

    _, l, alpha, _ = lax.fori_loop(0, n_chunks // unroll, body,
                                   (m, l, jnp.ones_like(l), _col_max(s)))
    last = (n_chunks - 1) * tk
    acc = alpha * acc_ref[...] + jnp.dot(vxt_ref[0, :, last:], p_b[...], preferred_element_type=F32)
    o_ref[0] = (acc / l).astype(BF16)


def _flash(qt, kc, vct, kx, vxt, *, heads):
    bsz, _, n = qt.shape
    nc = kc.shape[1]
    tq = min(512, n)
    tk = min(512, n // 2)
    unroll = 8 if (n // tk) % 8 == 0 else 2
    assert (n // tk) % unroll == 0
    return pl.pallas_call(
        functools.partial(_flash_kernel, tk=tk, unroll=unroll),
        grid=(bsz, heads, n // tq),
        in_specs=[
            pl.BlockSpec((1, HEAD_W, tq), lambda b, h, i: (b, h, i)),
            pl.BlockSpec((1, nc, HEAD_W), lambda b, h, i: (b, 0, h)),
            pl.BlockSpec((1, V_DIM, nc), lambda b, h, i: (b, h, 0)),
            pl.BlockSpec((1, n, HEAD_W), lambda b, h, i: (b, 0, h)),
            pl.BlockSpec((1, V_DIM, n), lambda b, h, i: (b, h, 0)),
        ],
        out_specs=pl.BlockSpec((1, V_DIM, tq), lambda b, h, i: (b, h, i)),
        out_shape=jax.ShapeDtypeStruct((bsz, heads * V_DIM, n), BF16),
        scratch_shapes=[pltpu.VMEM((tk, tq), F32), pltpu.VMEM((tk, tq), F32),
                        pltpu.VMEM((tk, tq), BF16), pltpu.VMEM((tk, tq), BF16),
                        pltpu.VMEM((V_DIM, tq), F32)],
        compiler_params=_cparams(("parallel", "parallel", "arbitrary")),
        name="flash",
    )(qt, kc, vct, kx, vxt)


def _attn_out_kernel(x_ref, at_ref, w_ref, mod_ref, o_ref):
    y = lax.dot_general(at_ref[0], w_ref[...], (((0,), (0,)), ((), ())),
                        preferred_element_type=F32)
    o_ref[0] = x_ref[0] + mod_ref[0, 5:6, :] * y


def _attn_out(x3, at, w, mod):
    bsz, n, d = x3.shape
    din = at.shape[1]
    tm = min(512, n)
    return pl.pallas_call(
        _attn_out_kernel,
        grid=(bsz, n // tm),
        in_specs=[
            pl.BlockSpec((1, tm, d), lambda b, i: (b, i, 0)),
            pl.BlockSpec((1, din, tm), lambda b, i: (b, 0, i)),
            pl.BlockSpec((din, d), lambda b, i: (0, 0)),
            pl.BlockSpec((1, N_MOD, d), lambda b, i: (b, 0, 0)),
        ],
        out_specs=pl.BlockSpec((1, tm, d), lambda b, i: (b, i, 0)),
        out_shape=jax.ShapeDtypeStruct((bsz, n, d), F32),
        compiler_params=_cparams(("parallel", "parallel")),
        name="attn_out",
    )(x3, at, w, mod)


def _rope_rot_cols(w):
    q = QK_ROPE // 4
    parts = w.reshape(*w.shape[:-1], 2, 2, q)
    rot = jnp.stack([-parts[..., 1, :], parts[..., 0, :]], axis=-2)
    return rot.reshape(w.shape)


def _pad_to(a, size, axis):
    pad = [(0, 0)] * a.ndim
    pad[axis] = (0, size - a.shape[axis])
    return jnp.pad(a, pad)


def _mla_weights(w_dq, q_norm, w_uq, w_dkv, kv_norm, w_ukv, heads):
    d, q_lora = w_dq.shape
    kv_lora = kv_norm.shape[0]
    qk_dim = QK_NOPE + QK_ROPE
    ql_pad = -(-q_lora // LANES) * LANES
    w_dq_p = _pad_to(w_dq, ql_pad, 1).astype(BF16)
    q_norm_p = _pad_to(q_norm, ql_pad, 0).reshape(1, ql_pad)
    uq = w_uq.reshape(q_lora, heads, qk_dim)
    uq_nope, uq_pe = uq[..., :QK_NOPE], uq[..., QK_NOPE:]
    zeros = jnp.zeros((q_lora, heads, LANES - QK_ROPE), w_uq.dtype)
    w_q = jnp.concatenate([uq_pe, zeros, uq_nope], axis=-1).reshape(q_lora, heads * HEAD_W)
    w_qr = jnp.concatenate([_rope_rot_cols(uq_pe), zeros], axis=-1).reshape(q_lora, heads * LANES)
    w_q = _pad_to(w_q, ql_pad, 0).astype(BF16)
    w_qr = _pad_to(w_qr, ql_pad, 0).astype(BF16)
    w_c = _pad_to(w_dkv[:, :kv_lora], 2 * LANES, 1)
    w_pe = w_dkv[:, kv_lora:]
    w_kv = jnp.concatenate([w_c, _pad_to(w_pe, LANES, 1), _pad_to(_rope_rot_cols(w_pe), LANES, 1)],
                           axis=1).astype(BF16)
    kv_norm_p = _pad_to(kv_norm, 2 * LANES, 0).reshape(1, 2 * LANES)
    ukv = w_ukv.reshape(kv_lora, heads, QK_NOPE + V_DIM)
    w_ukv_p = jnp.concatenate([ukv[..., :QK_NOPE].reshape(kv_lora, heads * QK_NOPE),
                               ukv[..., QK_NOPE:].reshape(kv_lora, heads * V_DIM)], axis=1)
    w_ukv_p = _pad_to(w_ukv_p, 2 * LANES, 0).astype(BF16)
    nope_w = heads * QK_NOPE
    w_uk, w_uvt = w_ukv_p[:, :nope_w], w_ukv_p[:, nope_w:].T
    return (w_dq_p, q_norm_p, w_q.T, w_qr.T, w_kv, kv_norm_p, w_uk, w_uvt), q_lora, kv_lora


def _rope_tables(n):
    t = jnp.arange(n)
    row = (t // GRID_W).astype(F32)
    col = (t % GRID_W).astype(F32)
    half = QK_ROPE // 4
    inv = ROPE_THETA ** (-jnp.arange(half, dtype=F32) / half)
    ang_r = row[:, None] * inv
    ang_c = col[:, None] * inv
    ang = jnp.concatenate([ang_r, ang_r, ang_c, ang_c], axis=-1)
    cos = _pad_to(jnp.cos(ang), LANES, 1)
    sin = _pad_to(jnp.sin(ang), LANES, 1)
    return cos, sin


def kernel(x, c, ctx, c_ctx, mod_w, mod_b, norm_g, ffn_w_gate, ffn_w_up, ffn_w_down,
           gm_w_in, gm_ln_g, gm_ln_b, gm_w_s, gm_b_s, gm_w_out,
           mla_w_dq, mla_q_norm, mla_w_uq, mla_w_dkv, mla_kv_norm, mla_w_ukv, mla_w_o,
           final_g):
    bsz, n, d = x.shape
    n_ctx = ctx.shape[1]
    depth = mod_w.shape[0]
    heads = d // V_DIM
    assert bsz + 1 <= 8

    cond8 = jnp.concatenate([c, c_ctx[None], jnp.zeros((8 - bsz - 1, d), F32)], axis=0)
    mods = _adaln(cond8, mod_w, mod_b).reshape(depth, 8, N_MOD, d)

    xt = x.reshape(bsz * n, d)
    yt = ctx.reshape(bsz * n_ctx, d)
    wg = ffn_w_gate.astype(BF16)
    wu = ffn_w_up.astype(BF16)
    wd = ffn_w_down.astype(BF16)

    for i in range(depth):
        last = i == depth - 1
        use_a = i % 2 == 0
        j = i // 2
        m_x = mods[i, :bsz]
        m_y = mods[i, bsz:bsz + 1]
        ctx_needed = (not last) or (not use_a)

        xt = _ffn(xt, m_x, 0, norm_g[i, 0], wg[i, 0], wu[i, 0], wd[i, 0])
        if ctx_needed:
            yt = _ffn(yt, m_y, 0, norm_g[i, 0], wg[i, 0], wu[i, 0], wd[i, 0])

        if use_a:
            inner = gm_w_out.shape[1]
            w_in = gm_w_in[j].astype(BF16)
            w_u, w_v = w_in[:, :inner], w_in[:, inner:]
            w_s = gm_w_s[j].astype(BF16)
            b_full = jnp.repeat(gm_b_s[j].T, inner // GM_GROUPS, axis=1)
            w_out = gm_w_out[j].astype(BF16)
            gm = (norm_g[i, 1], w_u, w_v, gm_ln_g[j], gm_ln_b[j], w_s, b_full)
            xt = _proj_res(xt, _gmlp_gate(xt, m_x, *gm), w_out, m_x)
            if not last:
                yt = _proj_res(yt, _gmlp_gate(yt, m_y, *gm), w_out, m_y)
        else:
            wts, q_lora, kv_lora = _mla_weights(mla_w_dq[j], mla_q_norm[j], mla_w_uq[j],
                                                mla_w_dkv[j], mla_kv_norm[j], mla_w_ukv[j], heads)
            q_scale = (QK_NOPE + QK_ROPE) ** -0.5 * math.log2(math.e)
            cos, sin = _rope_tables(n)
            ones = _pad_to(jnp.ones((n_ctx, QK_ROPE), F32), LANES, 1)
            zeros = jnp.zeros((n_ctx, LANES), F32)
            kw = dict(heads=heads, q_lora=q_lora, kv_lora=kv_lora, q_scale=q_scale)
            kc, vct = _mla_proj(yt.reshape(bsz, n_ctx, d), m_y, norm_g[i, 1], ones, zeros,
                                (None,) * 4 + wts[4:], with_q=False, **kw)
            qt, kx, vxt = _mla_proj(xt.reshape(bsz, n, d), m_x, norm_g[i, 1], cos, sin, wts,
                                    with_q=True, **kw)
            att_t = _flash(qt, kc, vct, kx, vxt, heads=heads)
            xt = _attn_out(xt.reshape(bsz, n, d), att_t, mla_w_o[j].astype(BF16), m_x)
            xt = xt.reshape(bsz * n, d)
            assert last, "context attention output is only needed for non-final MLA layers"

        xt = _ffn(xt, m_x, 2, norm_g[i, 2], wg[i, 1], wu[i, 1], wd[i, 1],
                  final_g=final_g if last else None)
        if not last:
            yt = _ffn(yt, m_y, 2, norm_g[i, 2], wg[i, 1], wu[i, 1], wd[i, 1])

    return xt.reshape(bsz, n, d)
```

```python
import functools
import math

import jax
import jax.numpy as jnp
from jax import lax
from jax.experimental import pallas as pl
from jax.experimental.pallas import tpu as pltpu

EPS = 1e-6
N_MOD = 9
GRID_W = 64
CHUNK = 128
GM_GROUPS = 8
QK_NOPE = 128
QK_ROPE = 64
V_DIM = 128
ROPE_THETA = 10000.0
HEAD_W = 256
LANES = 128
SUM_ROWS = 16
VT_ROWS = V_DIM + SUM_ROWS
SUBLANES = 8
MAX_CHAINS = 4
RING = 2
FLASH_UNROLL = 8
VMEM_LIMIT = 56 * 1024 * 1024
VMEM_LIMIT_FFN = 60 * 1024 * 1024

BF16 = jnp.bfloat16
F32 = jnp.float32


def _cparams(sem, vmem=VMEM_LIMIT, flags=None):
    return pltpu.CompilerParams(dimension_semantics=sem, vmem_limit_bytes=vmem, flags=flags)


def _rms(x, width=None):
    n = x.shape[-1] if width is None else width
    return x * lax.rsqrt(jnp.sum(x * x, axis=-1, keepdims=True) * (1.0 / n) + EPS)


def _premod(x, mod_ref, k, g):
    y = _rms(x) * g
    return y * (1.0 + mod_ref[0, 3 * k + 1:3 * k + 2, :]) + mod_ref[0, 3 * k:3 * k + 1, :]


def _adaln_kernel(c_ref, w_ref, b_ref, o_ref):
    c = c_ref[...]
    s = c * jax.nn.sigmoid(c)
    o_ref[0] = jnp.dot(s, w_ref[0], preferred_element_type=F32,
                       precision=lax.Precision.HIGHEST) + b_ref[0]


def _adaln(cond8, mod_w, mod_b):
    depth, d, n = mod_w.shape
    tn = min(d, 1024)
    assert n % tn == 0
    return pl.pallas_call(
        _adaln_kernel,
        grid=(depth, n // tn),
        in_specs=[
            pl.BlockSpec((8, d), lambda l, j: (0, 0)),
            pl.BlockSpec((1, d, tn), lambda l, j: (l, 0, j)),
            pl.BlockSpec((1, 1, tn), lambda l, j: (l, 0, j)),
        ],
        out_specs=pl.BlockSpec((1, 8, tn), lambda l, j: (l, 0, j)),
        out_shape=jax.ShapeDtypeStruct((depth, 8, n), F32),
        compiler_params=_cparams(("parallel", "parallel")),
        name="adaln",
    )(cond8, mod_w, mod_b.reshape(depth, 1, n))


def _ffn_kernel(*refs, k, final):
    if final:
        x_ref, mod_ref, g_ref, wg_ref, wu_ref, wd_ref, fg_ref, o_ref, h_ref = refs
    else:
        x_ref, mod_ref, g_ref, wg_ref, wu_ref, wd_ref, o_ref, h_ref = refs
    f = pl.program_id(1)

    @pl.when(f == 0)
    def _():
        x = x_ref[...]
        h_ref[...] = _premod(x, mod_ref, k, g_ref[...]).astype(BF16)
        o_ref[...] = x

    h = h_ref[...]
    a = jnp.dot(h, wg_ref[...], preferred_element_type=F32)
    b = jnp.dot(h, wu_ref[...], preferred_element_type=F32)
    p = (a * jax.nn.sigmoid(a) * b).astype(BF16)
    half_gate = 0.5 * mod_ref[0, 3 * k + 2:3 * k + 3, :]
    o_ref[...] += jnp.dot(p, wd_ref[...], preferred_element_type=F32) * half_gate

    if final:
        @pl.when(f == pl.num_programs(1) - 1)
        def _():
            o_ref[...] = _rms(o_ref[...]) * fg_ref[...]


def _ffn(x, mod, k, g, wg, wu, wd, final_g=None):
    t, d = x.shape
    nb = mod.shape[0]
    ff = wg.shape[1]
    tm = min(1024, t // nb)
    tf = min(256 if tm * d >= 1024 * 2048 else 512, ff)
    tiles_per_mod = (t // nb) // tm
    final = final_g is not None
    in_specs = [
        pl.BlockSpec((tm, d), lambda i, f: (i, 0)),
        pl.BlockSpec((1, N_MOD, d), lambda i, f: (i // tiles_per_mod, 0, 0)),
        pl.BlockSpec((1, d), lambda i, f: (0, 0)),
        pl.BlockSpec((d, tf), lambda i, f: (0, f)),
        pl.BlockSpec((d, tf), lambda i, f: (0, f)),
        pl.BlockSpec((tf, d), lambda i, f: (f, 0)),
    ]
    args = [x, mod, g.reshape(1, d), wg, wu, wd]
    if final:
        in_specs.append(pl.BlockSpec((1, d), lambda i, f: (0, 0)))
        args.append(final_g.reshape(1, d))
    return pl.pallas_call(
        functools.partial(_ffn_kernel, k=k, final=final),
        grid=(t // tm, ff // tf),
        in_specs=in_specs,
        out_specs=pl.BlockSpec((tm, d), lambda i, f: (i, 0)),
        out_shape=jax.ShapeDtypeStruct((t, d), F32),
        scratch_shapes=[pltpu.VMEM((tm, d), BF16)],
        compiler_params=_cparams(("parallel", "arbitrary"), VMEM_LIMIT_FFN),
        name="ffn",
    )(*args)


def _proj_res_kernel(x_ref, a_ref, w_ref, mod_ref, o_ref):
    y = jnp.dot(a_ref[...], w_ref[...], preferred_element_type=F32)
    o_ref[...] = x_ref[...] + mod_ref[0, 5:6, :] * y


def _proj_res(x, a, w, mod):
    t, d = x.shape
    nb = mod.shape[0]
    din = a.shape[1]
    tm = min(512, t // nb)
    tiles_per_mod = (t // nb) // tm
    return pl.pallas_call(
        _proj_res_kernel,
        grid=(t // tm,),
        in_specs=[
            pl.BlockSpec((tm, d), lambda i: (i, 0)),
            pl.BlockSpec((tm, din), lambda i: (i, 0)),
            pl.BlockSpec((din, d), lambda i: (0, 0)),
            pl.BlockSpec((1, N_MOD, d), lambda i: (i // tiles_per_mod, 0, 0)),
        ],
        out_specs=pl.BlockSpec((tm, d), lambda i: (i, 0)),
        out_shape=jax.ShapeDtypeStruct((t, d), F32),
        compiler_params=_cparams(("parallel",)),
        name="proj_res",
    )(x, a, w, mod)


def _gmlp_kernel(x_ref, mod_ref, g_ref, wu_ref, wv_ref, lng_ref, lnb_ref, ws_ref,
                 bs_ref, o_ref, v_ref):
    tm, inner = o_ref.shape
    gdim = inner // GM_GROUPS
    h = _premod(x_ref[...], mod_ref, 1, g_ref[...]).astype(BF16)
    v = jax.nn.gelu(jnp.dot(h, wv_ref[...], preferred_element_type=F32))
    mu = jnp.mean(v, axis=-1, keepdims=True)
    vc = v - mu
    var = jnp.mean(vc * vc, axis=-1, keepdims=True)
    v_ref[...] = ((vc * lax.rsqrt(var + EPS)) * lng_ref[...] + lnb_ref[...]).astype(BF16)
    for c in range(tm // CHUNK):
        rows = slice(c * CHUNK, (c + 1) * CHUNK)
        for gi in range(GM_GROUPS):
            cols = slice(gi * gdim, (gi + 1) * gdim)
            mixed = jnp.dot(ws_ref[gi], v_ref[rows, cols], preferred_element_type=F32)
            o_ref[rows, cols] = (mixed + bs_ref[:, cols]).astype(BF16)
    u = jax.nn.gelu(jnp.dot(h, wu_ref[...], preferred_element_type=F32))
    o_ref[...] = (u * o_ref[...].astype(F32)).astype(BF16)


def _gmlp_gate(x, mod, g, w_u, w_v, ln_g, ln_b, w_s, b_s_full):
    t, d = x.shape
    nb = mod.shape[0]
    inner = w_u.shape[1]
    tm = min(256, t // nb)
    tiles_per_mod = (t // nb) // tm
    const2 = lambda i: (0, 0)
    return pl.pallas_call(
        _gmlp_kernel,
        grid=(t // tm,),
        in_specs=[
            pl.BlockSpec((tm, d), lambda i: (i, 0)),
            pl.BlockSpec((1, N_MOD, d), lambda i: (i // tiles_per_mod, 0, 0)),
            pl.BlockSpec((1, d), const2),
            pl.BlockSpec((d, inner), const2),
            pl.BlockSpec((d, inner), const2),
            pl.BlockSpec((1, inner), const2),
            pl.BlockSpec((1, inner), const2),
            pl.BlockSpec((GM_GROUPS, CHUNK, CHUNK), lambda i: (0, 0, 0)),
            pl.BlockSpec((CHUNK, inner), const2),
        ],
        out_specs=pl.BlockSpec((tm, inner), lambda i: (i, 0)),
        out_shape=jax.ShapeDtypeStruct((t, inner), BF16),
        scratch_shapes=[pltpu.VMEM((tm, inner), BF16)],
        compiler_params=_cparams(("parallel",)),
        name="gmlp_gate",
    )(x, mod, g.reshape(1, d), w_u, w_v, ln_g.reshape(1, inner), ln_b.reshape(1, inner),
      w_s, b_s_full)


def _mla_proj_kernel(*refs, heads, q_lora, kv_lora, q_scale, with_q):
    if with_q:
        (x_ref, mod_ref, g_ref, cos_ref, sin_ref, cost_ref, sint_ref, wdq_ref, qn_ref, wqt_ref,
         wqrt_ref, wkv_ref, kvn_ref, wuk_ref, wuvt_ref, qt_ref, k_ref, vt_ref) = refs
    else:
        (x_ref, mod_ref, g_ref, cos_ref, sin_ref,
         wkv_ref, kvn_ref, wuk_ref, wuvt_ref, k_ref, vt_ref) = refs
    h = _premod(x_ref[0], mod_ref, 1, g_ref[...]).astype(BF16)

    kv = jnp.dot(h, wkv_ref[...], preferred_element_type=F32)
    c_kv = _rms(kv[:, :2 * LANES], kv_lora) * kvn_ref[...]
    k_pe = (kv[:, 2 * LANES:3 * LANES] * cos_ref[...] + kv[:, 3 * LANES:] * sin_ref[...]).astype(BF16)
    up_k = jnp.dot(c_kv.astype(BF16), wuk_ref[...], preferred_element_type=F32)
    for hd in range(heads):
        k_ref[0, :, hd * HEAD_W:hd * HEAD_W + LANES] = k_pe
        k_ref[0, :, hd * HEAD_W + LANES:(hd + 1) * HEAD_W] = (
            up_k[:, hd * QK_NOPE:(hd + 1) * QK_NOPE].astype(BF16))
    v_t = jnp.dot(wuvt_ref[...], c_kv.T.astype(BF16), preferred_element_type=F32).astype(BF16)
    ones = jnp.ones((SUM_ROWS, v_t.shape[1]), BF16)
    for hd in range(heads):
        vt_ref[0, hd * VT_ROWS:hd * VT_ROWS + V_DIM, :] = v_t[hd * V_DIM:(hd + 1) * V_DIM]
        vt_ref[0, hd * VT_ROWS + V_DIM:(hd + 1) * VT_ROWS, :] = ones

    if with_q:
        ql = jnp.dot(h, wdq_ref[...], preferred_element_type=F32)
        qn_t = (_rms(ql, q_lora) * qn_ref[...] * q_scale).T.astype(BF16)
        qa_t = jnp.dot(wqt_ref[...], qn_t, preferred_element_type=F32)
        qr_t = jnp.dot(wqrt_ref[...], qn_t, preferred_element_type=F32)
        cos_t = cost_ref[...]
        sin_t = sint_ref[...]
        for hd in range(heads):
            pe = (qa_t[hd * HEAD_W:hd * HEAD_W + LANES] * cos_t
                  + qr_t[hd * LANES:(hd + 1) * LANES] * sin_t)
            qt_ref[0, hd * HEAD_W:hd * HEAD_W + LANES, :] = pe.astype(BF16)
            qt_ref[0, hd * HEAD_W + LANES:(hd + 1) * HEAD_W, :] = (
                qa_t[hd * HEAD_W + LANES:(hd + 1) * HEAD_W].astype(BF16))


def _mla_proj(x3, mod, g, cos, sin, wts, *, heads, q_lora, kv_lora, q_scale, with_q):
    bsz, n, d = x3.shape
    nb = mod.shape[0]
    tm = min(256, n)
    mod_idx = (lambda b, i: (b, 0, 0)) if nb == bsz else (lambda b, i: (0, 0, 0))
    const2 = lambda b, i: (0, 0)
    w_dq, q_norm, w_qt, w_qrt, w_kv, kv_norm, w_uk, w_uvt = wts
    in_specs = [
        pl.BlockSpec((1, tm, d), lambda b, i: (b, i, 0)),
        pl.BlockSpec((1, N_MOD, d), mod_idx),
        pl.BlockSpec((1, d), const2),
        pl.BlockSpec((tm, LANES), lambda b, i: (i, 0)),
        pl.BlockSpec((tm, LANES), lambda b, i: (i, 0)),
    ]
    args = [x3, mod, g.reshape(1, d), cos, sin]
    if with_q:
        in_specs += [pl.BlockSpec((LANES, tm), lambda b, i: (0, i)),
                     pl.BlockSpec((LANES, tm), lambda b, i: (0, i))]
        args += [cos.T, sin.T]
        in_specs += [pl.BlockSpec(w.shape, const2) for w in (w_dq, q_norm, w_qt, w_qrt)]
        args += [w_dq, q_norm, w_qt, w_qrt]
    in_specs += [pl.BlockSpec(w.shape, const2) for w in (w_kv, kv_norm, w_uk, w_uvt)]
    args += [w_kv, kv_norm, w_uk, w_uvt]
    out_specs = [pl.BlockSpec((1, tm, heads * HEAD_W), lambda b, i: (b, i, 0)),
                 pl.BlockSpec((1, heads * VT_ROWS, tm), lambda b, i: (b, 0, i))]
    out_shape = [jax.ShapeDtypeStruct((bsz, n, heads * HEAD_W), BF16),
                 jax.ShapeDtypeStruct((bsz, heads * VT_ROWS, n), BF16)]
    if with_q:
        out_specs = [pl.BlockSpec((1, heads * HEAD_W, tm), lambda b, i: (b, 0, i))] + out_specs
        out_shape = [jax.ShapeDtypeStruct((bsz, heads * HEAD_W, n), BF16)] + out_shape
    return pl.pallas_call(
        functools.partial(_mla_proj_kernel, heads=heads, q_lora=q_lora, kv_lora=kv_lora,
                          q_scale=q_scale, with_q=with_q),
        grid=(bsz, n // tm),
        in_specs=in_specs,
        out_specs=out_specs,
        out_shape=out_shape,
        compiler_params=_cparams(("parallel", "parallel")),
        name="mla_proj_q" if with_q else "mla_proj_kv",
    )(*args)


def _col_max(s):
    rows = s.shape[0]
    parts = [s[r * SUBLANES:(r + 1) * SUBLANES] for r in range(MAX_CHAINS)]
    for r in range(MAX_CHAINS, rows // SUBLANES):
        parts[r % MAX_CHAINS] = jnp.maximum(parts[r % MAX_CHAINS], s[r * SUBLANES:(r + 1) * SUBLANES])
    while len(parts) > 1:
        parts = [jnp.maximum(parts[i], parts[i + 1]) for i in range(0, len(parts), 2)]
    return jnp.max(parts[0], axis=0, keepdims=True)


def _flash_kernel(qt_ref, kc_ref, vct_ref, kx_ref, vxt_ref, o_ref, *scratch, tk, unroll):
    s_bufs, p_bufs, acc_ref = scratch[:RING], scratch[RING:2 * RING], scratch[2 * RING]
    qt = qt_ref[0]
    n_chunks = kx_ref.shape[1] // tk

    def pv(vt, p):
        return jnp.dot(vt, p, preferred_element_type=F32)

    def probs(s, m):
        return jnp.exp2((s - m).astype(BF16))

    s = jnp.dot(kc_ref[0], qt, preferred_element_type=F32)
    m = _col_max(s)
    acc_ref[...] = pv(vct_ref[0], probs(s, m))
    s = jnp.dot(kx_ref[0, :tk, :], qt, preferred_element_type=F32)
    s_bufs[0][...] = s
    p_bufs[RING - 1][...] = jnp.zeros_like(p_bufs[RING - 1])

    def step(j, u, carry):
        m, alpha_prev, cmax = carry
        nxt = pl.multiple_of(jnp.minimum(j + 1, n_chunks - 1) * tk, tk)
        prv = pl.multiple_of(jnp.maximum(j - 1, 0) * tk, tk)
        m_new = jnp.maximum(m, cmax)
        p_bufs[u][...] = probs(s_bufs[u][...], m_new)
        acc_ref[...] = alpha_prev * acc_ref[...] + pv(vxt_ref[0, :, pl.ds(prv, tk)],
                                                      p_bufs[(u - 1) % RING][...])
        s_next = jnp.dot(kx_ref[0, pl.ds(nxt, tk), :], qt, preferred_element_type=F32)
        s_bufs[(u + 1) % RING][...] = s_next
        return m_new, jnp.exp2(m - m_new), _col_max(s_next)

    def body(i, carry):
        for u in range(unroll):
            carry = step(unroll * i + u, u % RING, carry)
        return carry

    _, alpha, _ = lax.fori_loop(0, n_chunks // unroll, body, (m, jnp.ones_like(m), _col_max(s)))
    last = (n_chunks - 1) * tk
    acc = alpha * acc_ref[...] + pv(vxt_ref[0, :, last:], p_bufs[RING - 1][...])
    o_ref[0] = (acc[:V_DIM] / acc[V_DIM:V_DIM + 1]).astype(BF16)


def _flash(qt, kc, vct, kx, vxt, *, heads):
    bsz, _, n = qt.shape
    nc = kc.shape[1]
    tq = min(512, n)
    tk = min(512, n // RING)
    unroll = FLASH_UNROLL if (n // tk) % FLASH_UNROLL == 0 else RING
    assert (n // tk) % unroll == 0
    return pl.pallas_call(
        functools.partial(_flash_kernel, tk=tk, unroll=unroll),
        grid=(bsz, heads, n // tq),
        in_specs=[
            pl.BlockSpec((1, HEAD_W, tq), lambda b, h, i: (b, h, i)),
            pl.BlockSpec((1, nc, HEAD_W), lambda b, h, i: (b, 0, h)),
            pl.BlockSpec((1, VT_ROWS, nc), lambda b, h, i: (b, h, 0)),
            pl.BlockSpec((1, n, HEAD_W), lambda b, h, i: (b, 0, h)),
            pl.BlockSpec((1, VT_ROWS, n), lambda b, h, i: (b, h, 0)),
        ],
        out_specs=pl.BlockSpec((1, V_DIM, tq), lambda b, h, i: (b, h, i)),
        out_shape=jax.ShapeDtypeStruct((bsz, heads * V_DIM, n), BF16),
        scratch_shapes=([pltpu.VMEM((tk, tq), F32)] * RING + [pltpu.VMEM((tk, tq), BF16)] * RING
                        + [pltpu.VMEM((VT_ROWS, tq), F32)]),
        compiler_params=_cparams(("parallel", "parallel", "arbitrary")),
        name="flash",
    )(qt, kc, vct, kx, vxt)


def _attn_out_kernel(x_ref, at_ref, w_ref, mod_ref, o_ref):
    y = lax.dot_general(at_ref[0], w_ref[...], (((0,), (0,)), ((), ())),
                        preferred_element_type=F32)
    o_ref[0] = x_ref[0] + mod_ref[0, 5:6, :] * y


def _attn_out(x3, at, w, mod):
    bsz, n, d = x3.shape
    din = at.shape[1]
    tm = min(512, n)
    return pl.pallas_call(
        _attn_out_kernel,
        grid=(bsz, n // tm),
        in_specs=[
            pl.BlockSpec((1, tm, d), lambda b, i: (b, i, 0)),
            pl.BlockSpec((1, din, tm), lambda b, i: (b, 0, i)),
            pl.BlockSpec((din, d), lambda b, i: (0, 0)),
            pl.BlockSpec((1, N_MOD, d), lambda b, i: (b, 0, 0)),
        ],
        out_specs=pl.BlockSpec((1, tm, d), lambda b, i: (b, i, 0)),
        out_shape=jax.ShapeDtypeStruct((bsz, n, d), F32),
        compiler_params=_cparams(("parallel", "parallel")),
        name="attn_out",
    )(x3, at, w, mod)


def _rope_rot_cols(w):
    q = QK_ROPE // 4
    parts = w.reshape(*w.shape[:-1], 2, 2, q)
    rot = jnp.stack([-parts[..., 1, :], parts[..., 0, :]], axis=-2)
    return rot.reshape(w.shape)


def _pad_to(a, size, axis):
    pad = [(0, 0)] * a.ndim
    pad[axis] = (0, size - a.shape[axis])
    return jnp.pad(a, pad)


def _mla_weights(w_dq, q_norm, w_uq, w_dkv, kv_norm, w_ukv, heads):
    d, q_lora = w_dq.shape
    kv_lora = kv_norm.shape[0]
    qk_dim = QK_NOPE + QK_ROPE
    ql_pad = -(-q_lora // LANES) * LANES
    w_dq_p = _pad_to(w_dq, ql_pad, 1).astype(BF16)
    q_norm_p = _pad_to(q_norm, ql_pad, 0).reshape(1, ql_pad)
    uq = w_uq.reshape(q_lora, heads, qk_dim)
    uq_nope, uq_pe = uq[..., :QK_NOPE], uq[..., QK_NOPE:]
    zeros = jnp.zeros((q_lora, heads, LANES - QK_ROPE), w_uq.dtype)
    w_q = jnp.concatenate([uq_pe, zeros, uq_nope], axis=-1).reshape(q_lora, heads * HEAD_W)
    w_qr = jnp.concatenate([_rope_rot_cols(uq_pe), zeros], axis=-1).reshape(q_lora, heads * LANES)
    w_q = _pad_to(w_q, ql_pad, 0).astype(BF16)
    w_qr = _pad_to(w_qr, ql_pad, 0).astype(BF16)
    w_c = _pad_to(w_dkv[:, :kv_lora], 2 * LANES, 1)
    w_pe = w_dkv[:, kv_lora:]
    w_kv = jnp.concatenate([w_c, _pad_to(w_pe, LANES, 1), _pad_to(_rope_rot_cols(w_pe), LANES, 1)],
                           axis=1).astype(BF16)
    kv_norm_p = _pad_to(kv_norm, 2 * LANES, 0).reshape(1, 2 * LANES)
    ukv = w_ukv.reshape(kv_lora, heads, QK_NOPE + V_DIM)
    w_ukv_p = jnp.concatenate([ukv[..., :QK_NOPE].reshape(kv_lora, heads * QK_NOPE),
                               ukv[..., QK_NOPE:].reshape(kv_lora, heads * V_DIM)], axis=1)
    w_ukv_p = _pad_to(w_ukv_p, 2 * LANES, 0).astype(BF16)
    nope_w = heads * QK_NOPE
    w_uk, w_uvt = w_ukv_p[:, :nope_w], w_ukv_p[:, nope_w:].T
    return (w_dq_p, q_norm_p, w_q.T, w_qr.T, w_kv, kv_norm_p, w_uk, w_uvt), q_lora, kv_lora


def _rope_tables(n):
    t = jnp.arange(n)
    row = (t // GRID_W).astype(F32)
    col = (t % GRID_W).astype(F32)
    half = QK_ROPE // 4
    inv = ROPE_THETA ** (-jnp.arange(half, dtype=F32) / half)
    ang_r = row[:, None] * inv
    ang_c = col[:, None] * inv
    ang = jnp.concatenate([ang_r, ang_r, ang_c, ang_c], axis=-1)
    cos = _pad_to(jnp.cos(ang), LANES, 1)
    sin = _pad_to(jnp.sin(ang), LANES, 1)
    return cos, sin


def kernel(x, c, ctx, c_ctx, mod_w, mod_b, norm_g, ffn_w_gate, ffn_w_up, ffn_w_down,
           gm_w_in, gm_ln_g, gm_ln_b, gm_w_s, gm_b_s, gm_w_out,
           mla_w_dq, mla_q_norm, mla_w_uq, mla_w_dkv, mla_kv_norm, mla_w_ukv, mla_w_o,
           final_g):
    bsz, n, d = x.shape
    n_ctx = ctx.shape[1]
    depth = mod_w.shape[0]
    heads = d // V_DIM
    assert bsz + 1 <= 8

    cond8 = jnp.concatenate([c, c_ctx[None], jnp.zeros((8 - bsz - 1, d), F32)], axis=0)
    mods = _adaln(cond8, mod_w, mod_b).reshape(depth, 8, N_MOD, d)

    xt = x.reshape(bsz * n, d)
    yt = ctx.reshape(bsz * n_ctx, d)
    wg = ffn_w_gate.astype(BF16)
    wu = ffn_w_up.astype(BF16)
    wd = ffn_w_down.astype(BF16)

    for i in range(depth):
        last = i == depth - 1
        use_a = i % 2 == 0
        j = i // 2
        m_x = mods[i, :bsz]
        m_y = mods[i, bsz:bsz + 1]
        ctx_needed = (not last) or (not use_a)

        xt = _ffn(xt, m_x, 0, norm_g[i, 0], wg[i, 0], wu[i, 0], wd[i, 0])
        if ctx_needed:
            yt = _ffn(yt, m_y, 0, norm_g[i, 0], wg[i, 0], wu[i, 0], wd[i, 0])

        if use_a:
            inner = gm_w_out.shape[1]
            w_in = gm_w_in[j].astype(BF16)
            w_u, w_v = w_in[:, :inner], w_in[:, inner:]
            w_s = gm_w_s[j].astype(BF16)
            b_full = jnp.repeat(gm_b_s[j].T, inner // GM_GROUPS, axis=1)
            w_out = gm_w_out[j].astype(BF16)
            gm = (norm_g[i, 1], w_u, w_v, gm_ln_g[j], gm_ln_b[j], w_s, b_full)
            xt = _proj_res(xt, _gmlp_gate(xt, m_x, *gm), w_out, m_x)
            if not last:
                yt = _proj_res(yt, _gmlp_gate(yt, m_y, *gm), w_out, m_y)
        else:
            wts, q_lora, kv_lora = _mla_weights(mla_w_dq[j], mla_q_norm[j], mla_w_uq[j],
                                                mla_w_dkv[j], mla_kv_norm[j], mla_w_ukv[j], heads)
            q_scale = (QK_NOPE + QK_ROPE) ** -0.5 * math.log2(math.e)
            cos, sin = _rope_tables(n)
            ones = _pad_to(jnp.ones((n_ctx, QK_ROPE), F32), LANES, 1)
            zeros = jnp.zeros((n_ctx, LANES), F32)
            kw = dict(heads=heads, q_lora=q_lora, kv_lora=kv_lora, q_scale=q_scale)
            kc, vct = _mla_proj(yt.reshape(bsz, n_ctx, d), m_y, norm_g[i, 1], ones, zeros,
                                (None,) * 4 + wts[4:], with_q=False, **kw)
            qt, kx, vxt = _mla_proj(xt.reshape(bsz, n, d), m_x, norm_g[i, 1], cos, sin, wts,
                                    with_q=True, **kw)
            att_t = _flash(qt, kc, vct, kx, vxt, heads=heads)
            xt = _attn_out(xt.reshape(bsz, n, d), att_t, mla_w_o[j].astype(BF16), m_x)
            xt = xt.reshape(bsz * n, d)
            assert last, "context attention output is only needed for non-final MLA layers"

        xt = _ffn(xt, m_x, 2, norm_g[i, 2], wg[i, 1], wu[i, 1], wd[i, 1],
                  final_g=final_g if last else None)
        if not last:
            yt = _ffn(yt, m_y, 2, norm_g[i, 2], wg[i, 1], wu[i, 1], wd[i, 1])

    return xt.reshape(bsz, n, d)
```

```python
import functools
import math

import jax
import jax.numpy as jnp
from jax import lax
from jax.experimental import pallas as pl
from jax.experimental.pallas import tpu as pltpu

EPS = 1e-6
N_MOD = 9
GRID_W = 64
CHUNK = 128
GM_GROUPS = 8
QK_NOPE = 128
QK_ROPE = 64
V_DIM = 128
ROPE_THETA = 10000.0
HEAD_W = 256
LANES = 128
SUM_ROWS = 16
VT_ROWS = V_DIM + SUM_ROWS
SUBLANES = 8
MAX_CHAINS = 4
RING = 4
LOOKAHEAD = 2
FLASH_UNROLL = 8
VMEM_LIMIT = 56 * 1024 * 1024
VMEM_LIMIT_FFN = 60 * 1024 * 1024

BF16 = jnp.bfloat16
F32 = jnp.float32


def _cparams(sem, vmem=VMEM_LIMIT, flags=None):
    return pltpu.CompilerParams(dimension_semantics=sem, vmem_limit_bytes=vmem, flags=flags)


def _rms(x, width=None):
    n = x.shape[-1] if width is None else width
    return x * lax.rsqrt(jnp.sum(x * x, axis=-1, keepdims=True) * (1.0 / n) + EPS)


def _premod(x, mod_ref, k, g):
    y = _rms(x) * g
    return y * (1.0 + mod_ref[0, 3 * k + 1:3 * k + 2, :]) + mod_ref[0, 3 * k:3 * k + 1, :]


def _adaln_kernel(c_ref, w_ref, b_ref, o_ref):
    c = c_ref[...]
    s = c * jax.nn.sigmoid(c)
    o_ref[0] = jnp.dot(s, w_ref[0], preferred_element_type=F32,
                       precision=lax.Precision.HIGHEST) + b_ref[0]


def _adaln(cond8, mod_w, mod_b):
    depth, d, n = mod_w.shape
    tn = min(d, 1024)
    assert n % tn == 0
    return pl.pallas_call(
        _adaln_kernel,
        grid=(depth, n // tn),
        in_specs=[
            pl.BlockSpec((8, d), lambda l, j: (0, 0)),
            pl.BlockSpec((1, d, tn), lambda l, j: (l, 0, j)),
            pl.BlockSpec((1, 1, tn), lambda l, j: (l, 0, j)),
        ],
        out_specs=pl.BlockSpec((1, 8, tn), lambda l, j: (l, 0, j)),
        out_shape=jax.ShapeDtypeStruct((depth, 8, n), F32),
        compiler_params=_cparams(("parallel", "parallel")),
        name="adaln",
    )(cond8, mod_w, mod_b.reshape(depth, 1, n))


def _ffn_kernel(*refs, k, final):
    if final:
        x_ref, mod_ref, g_ref, wg_ref, wu_ref, wd_ref, fg_ref, o_ref, h_ref = refs
    else:
        x_ref, mod_ref, g_ref, wg_ref, wu_ref, wd_ref, o_ref, h_ref = refs
    f = pl.program_id(1)

    @pl.when(f == 0)
    def _():
        x = x_ref[...]
        h_ref[...] = _premod(x, mod_ref, k, g_ref[...]).astype(BF16)
        o_ref[...] = x

    h = h_ref[...]
    a = jnp.dot(h, wg_ref[...], preferred_element_type=F32)
    b = jnp.dot(h, wu_ref[...], preferred_element_type=F32)
    p = (a * jax.nn.sigmoid(a) * b).astype(BF16)
    half_gate = 0.5 * mod_ref[0, 3 * k + 2:3 * k + 3, :]
    o_ref[...] += jnp.dot(p, wd_ref[...], preferred_element_type=F32) * half_gate

    if final:
        @pl.when(f == pl.num_programs(1) - 1)
        def _():
            o_ref[...] = _rms(o_ref[...]) * fg_ref[...]


def _ffn(x, mod, k, g, wg, wu, wd, final_g=None):
    t, d = x.shape
    nb = mod.shape[0]
    ff = wg.shape[1]
    tm = min(1024, t // nb)
    tf = min(256 if tm * d >= 1024 * 2048 else 512, ff)
    tiles_per_mod = (t // nb) // tm
    final = final_g is not None
    in_specs = [
        pl.BlockSpec((tm, d), lambda i, f: (i, 0)),
        pl.BlockSpec((1, N_MOD, d), lambda i, f: (i // tiles_per_mod, 0, 0)),
        pl.BlockSpec((1, d), lambda i, f: (0, 0)),
        pl.BlockSpec((d, tf), lambda i, f: (0, f)),
        pl.BlockSpec((d, tf), lambda i, f: (0, f)),
        pl.BlockSpec((tf, d), lambda i, f: (f, 0)),
    ]
    args = [x, mod, g.reshape(1, d), wg, wu, wd]
    if final:
        in_specs.append(pl.BlockSpec((1, d), lambda i, f: (0, 0)))
        args.append(final_g.reshape(1, d))
    return pl.pallas_call(
        functools.partial(_ffn_kernel, k=k, final=final),
        grid=(t // tm, ff // tf),
        in_specs=in_specs,
        out_specs=pl.BlockSpec((tm, d), lambda i, f: (i, 0)),
        out_shape=jax.ShapeDtypeStruct((t, d), F32),
        scratch_shapes=[pltpu.VMEM((tm, d), BF16)],
        compiler_params=_cparams(("parallel", "arbitrary"), VMEM_LIMIT_FFN),
        name="ffn",
    )(*args)


def _proj_res_kernel(x_ref, a_ref, w_ref, mod_ref, o_ref):
    y = jnp.dot(a_ref[...], w_ref[...], preferred_element_type=F32)
    o_ref[...] = x_ref[...] + mod_ref[0, 5:6, :] * y


def _proj_res(x, a, w, mod):
    t, d = x.shape
    nb = mod.shape[0]
    din = a.shape[1]
    tm = min(512, t // nb)
    tiles_per_mod = (t // nb) // tm
    return pl.pallas_call(
        _proj_res_kernel,
        grid=(t // tm,),
        in_specs=[
            pl.BlockSpec((tm, d), lambda i: (i, 0)),
            pl.BlockSpec((tm, din), lambda i: (i, 0)),
            pl.BlockSpec((din, d), lambda i: (0, 0)),
            pl.BlockSpec((1, N_MOD, d), lambda i: (i // tiles_per_mod, 0, 0)),
        ],
        out_specs=pl.BlockSpec((tm, d), lambda i: (i, 0)),
        out_shape=jax.ShapeDtypeStruct((t, d), F32),
        compiler_params=_cparams(("parallel",)),
        name="proj_res",
    )(x, a, w, mod)


def _gmlp_kernel(x_ref, mod_ref, g_ref, wu_ref, wv_ref, lng_ref, lnb_ref, ws_ref,
                 bs_ref, o_ref, v_ref):
    tm, inner = o_ref.shape
    gdim = inner // GM_GROUPS
    h = _premod(x_ref[...], mod_ref, 1, g_ref[...]).astype(BF16)
    v = jax.nn.gelu(jnp.dot(h, wv_ref[...], preferred_element_type=F32))
    mu = jnp.mean(v, axis=-1, keepdims=True)
    vc = v - mu
    var = jnp.mean(vc * vc, axis=-1, keepdims=True)
    v_ref[...] = ((vc * lax.rsqrt(var + EPS)) * lng_ref[...] + lnb_ref[...]).astype(BF16)
    for c in range(tm // CHUNK):
        rows = slice(c * CHUNK, (c + 1) * CHUNK)
        for gi in range(GM_GROUPS):
            cols = slice(gi * gdim, (gi + 1) * gdim)
            mixed = jnp.dot(ws_ref[gi], v_ref[rows, cols], preferred_element_type=F32)
            o_ref[rows, cols] = (mixed + bs_ref[:, cols]).astype(BF16)
    u = jax.nn.gelu(jnp.dot(h, wu_ref[...], preferred_element_type=F32))
    o_ref[...] = (u * o_ref[...].astype(F32)).astype(BF16)


def _gmlp_gate(x, mod, g, w_u, w_v, ln_g, ln_b, w_s, b_s_full):
    t, d = x.shape
    nb = mod.shape[0]
    inner = w_u.shape[1]
    tm = min(256, t // nb)
    tiles_per_mod = (t // nb) // tm
    const2 = lambda i: (0, 0)
    return pl.pallas_call(
        _gmlp_kernel,
        grid=(t // tm,),
        in_specs=[
            pl.BlockSpec((tm, d), lambda i: (i, 0)),
            pl.BlockSpec((1, N_MOD, d), lambda i: (i // tiles_per_mod, 0, 0)),
            pl.BlockSpec((1, d), const2),
            pl.BlockSpec((d, inner), const2),
            pl.BlockSpec((d, inner), const2),
            pl.BlockSpec((1, inner), const2),
            pl.BlockSpec((1, inner), const2),
            pl.BlockSpec((GM_GROUPS, CHUNK, CHUNK), lambda i: (0, 0, 0)),
            pl.BlockSpec((CHUNK, inner), const2),
        ],
        out_specs=pl.BlockSpec((tm, inner), lambda i: (i, 0)),
        out_shape=jax.ShapeDtypeStruct((t, inner), BF16),
        scratch_shapes=[pltpu.VMEM((tm, inner), BF16)],
        compiler_params=_cparams(("parallel",)),
        name="gmlp_gate",
    )(x, mod, g.reshape(1, d), w_u, w_v, ln_g.reshape(1, inner), ln_b.reshape(1, inner),
      w_s, b_s_full)


def _mla_proj_kernel(*refs, heads, q_lora, kv_lora, q_scale, with_q):
    if with_q:
        (x_ref, mod_ref, g_ref, cos_ref, sin_ref, cost_ref, sint_ref, wdq_ref, qn_ref, wqt_ref,
         wqrt_ref, wkv_ref, kvn_ref, wuk_ref, wuvt_ref, qt_ref, k_ref, vt_ref) = refs
    else:
        (x_ref, mod_ref, g_ref, cos_ref, sin_ref,
         wkv_ref, kvn_ref, wuk_ref, wuvt_ref, k_ref, vt_ref) = refs
    h = _premod(x_ref[0], mod_ref, 1, g_ref[...]).astype(BF16)

    kv = jnp.dot(h, wkv_ref[...], preferred_element_type=F32)
    c_kv = _rms(kv[:, :2 * LANES], kv_lora) * kvn_ref[...]
    k_pe = (kv[:, 2 * LANES:3 * LANES] * cos_ref[...] + kv[:, 3 * LANES:] * sin_ref[...]).astype(BF16)
    up_k = jnp.dot(c_kv.astype(BF16), wuk_ref[...], preferred_element_type=F32)
    for hd in range(heads):
        k_ref[0, :, hd * HEAD_W:hd * HEAD_W + LANES] = k_pe
        k_ref[0, :, hd * HEAD_W + LANES:(hd + 1) * HEAD_W] = (
            up_k[:, hd * QK_NOPE:(hd + 1) * QK_NOPE].astype(BF16))
    v_t = jnp.dot(wuvt_ref[...], c_kv.T.astype(BF16), preferred_element_type=F32).astype(BF16)
    ones = jnp.ones((SUM_ROWS, v_t.shape[1]), BF16)
    for hd in range(heads):
        vt_ref[0, hd * VT_ROWS:hd * VT_ROWS + V_DIM, :] = v_t[hd * V_DIM:(hd + 1) * V_DIM]
        vt_ref[0, hd * VT_ROWS + V_DIM:(hd + 1) * VT_ROWS, :] = ones

    if with_q:
        ql = jnp.dot(h, wdq_ref[...], preferred_element_type=F32)
        qn_t = (_rms(ql, q_lora) * qn_ref[...] * q_scale).T.astype(BF16)
        qa_t = jnp.dot(wqt_ref[...], qn_t, preferred_element_type=F32)
        qr_t = jnp.dot(wqrt_ref[...], qn_t, preferred_element_type=F32)
        cos_t = cost_ref[...]
        sin_t = sint_ref[...]
        for hd in range(heads):
            pe = (qa_t[hd * HEAD_W:hd * HEAD_W + LANES] * cos_t
                  + qr_t[hd * LANES:(hd + 1) * LANES] * sin_t)
            qt_ref[0, hd * HEAD_W:hd * HEAD_W + LANES, :] = pe.astype(BF16)
            qt_ref[0, hd * HEAD_W + LANES:(hd + 1) * HEAD_W, :] = (
                qa_t[hd * HEAD_W + LANES:(hd + 1) * HEAD_W].astype(BF16))


def _mla_proj(x3, mod, g, cos, sin, wts, *, heads, q_lora, kv_lora, q_scale, with_q):
    bsz, n, d = x3.shape
    nb = mod.shape[0]
    tm = min(256, n)
    mod_idx = (lambda b, i: (b, 0, 0)) if nb == bsz else (lambda b, i: (0, 0, 0))
    const2 = lambda b, i: (0, 0)
    w_dq, q_norm, w_qt, w_qrt, w_kv, kv_norm, w_uk, w_uvt = wts
    in_specs = [
        pl.BlockSpec((1, tm, d), lambda b, i: (b, i, 0)),
        pl.BlockSpec((1, N_MOD, d), mod_idx),
        pl.BlockSpec((1, d), const2),
        pl.BlockSpec((tm, LANES), lambda b, i: (i, 0)),
        pl.BlockSpec((tm, LANES), lambda b, i: (i, 0)),
    ]
    args = [x3, mod, g.reshape(1, d), cos, sin]
    if with_q:
        in_specs += [pl.BlockSpec((LANES, tm), lambda b, i: (0, i)),
                     pl.BlockSpec((LANES, tm), lambda b, i: (0, i))]
        args += [cos.T, sin.T]
        in_specs += [pl.BlockSpec(w.shape, const2) for w in (w_dq, q_norm, w_qt, w_qrt)]
        args += [w_dq, q_norm, w_qt, w_qrt]
    in_specs += [pl.BlockSpec(w.shape, const2) for w in (w_kv, kv_norm, w_uk, w_uvt)]
    args += [w_kv, kv_norm, w_uk, w_uvt]
    out_specs = [pl.BlockSpec((1, tm, heads * HEAD_W), lambda b, i: (b, i, 0)),
                 pl.BlockSpec((1, heads * VT_ROWS, tm), lambda b, i: (b, 0, i))]
    out_shape = [jax.ShapeDtypeStruct((bsz, n, heads * HEAD_W), BF16),
                 jax.ShapeDtypeStruct((bsz, heads * VT_ROWS, n), BF16)]
    if with_q:
        out_specs = [pl.BlockSpec((1, heads * HEAD_W, tm), lambda b, i: (b, 0, i))] + out_specs
        out_shape = [jax.ShapeDtypeStruct((bsz, heads * HEAD_W, n), BF16)] + out_shape
    return pl.pallas_call(
        functools.partial(_mla_proj_kernel, heads=heads, q_lora=q_lora, kv_lora=kv_lora,
                          q_scale=q_scale, with_q=with_q),
        grid=(bsz, n // tm),
        in_specs=in_specs,
        out_specs=out_specs,
        out_shape=out_shape,
        compiler_params=_cparams(("parallel", "parallel")),
        name="mla_proj_q" if with_q else "mla_proj_kv",
    )(*args)


def _col_max(s):
    rows = s.shape[0]
    parts = [s[r * SUBLANES:(r + 1) * SUBLANES] for r in range(MAX_CHAINS)]
    for r in range(MAX_CHAINS, rows // SUBLANES):
        parts[r % MAX_CHAINS] = jnp.maximum(parts[r % MAX_CHAINS], s[r * SUBLANES:(r + 1) * SUBLANES])
    while len(parts) > 1:
        parts = [jnp.maximum(parts[i], parts[i + 1]) for i in range(0, len(parts), 2)]
    return jnp.max(parts[0], axis=0, keepdims=True)


def _flash_kernel(qt_ref, kc_ref, vct_ref, kx_ref, vxt_ref, o_ref, *scratch, tk, unroll):
    s_bufs, acc_ref = scratch[:RING], scratch[RING]
    qt = qt_ref[0]
    n_chunks = kx_ref.shape[1] // tk

    def pv(vt, p):
        return jnp.dot(vt, p, preferred_element_type=F32)

    def probs(s, m):
        return jnp.exp2((s - m).astype(BF16))

    s = jnp.dot(kc_ref[0], qt, preferred_element_type=F32)
    m = _col_max(s)
    acc_ref[...] = pv(vct_ref[0], probs(s, m))
    cmaxes = []
    for c in range(LOOKAHEAD):
        s = jnp.dot(kx_ref[0, c * tk:(c + 1) * tk, :], qt, preferred_element_type=F32)
        s_bufs[c][...] = s
        cmaxes.append(_col_max(s))

    def step(j, u, carry):
        m, cmax, *cmax_ahead = carry
        cur = pl.multiple_of(j * tk, tk)
        nxt = pl.multiple_of(jnp.minimum(j + LOOKAHEAD, n_chunks - 1) * tk, tk)
        m_new = jnp.maximum(m, cmax)
        p = probs(s_bufs[u][...], m_new)
        acc_ref[...] = jnp.exp2(m - m_new) * acc_ref[...] + pv(vxt_ref[0, :, pl.ds(cur, tk)], p)
        s_next = jnp.dot(kx_ref[0, pl.ds(nxt, tk), :], qt, preferred_element_type=F32)
        s_bufs[(u + LOOKAHEAD) % RING][...] = s_next
        return (m_new, *cmax_ahead, _col_max(s_next))

    def body(i, carry):
        for u in range(unroll):
            carry = step(unroll * i + u, u % RING, carry)
        return carry

    lax.fori_loop(0, n_chunks // unroll, body, (m, *cmaxes))
    acc = acc_ref[...]
    o_ref[0] = (acc[:V_DIM] / acc[V_DIM:V_DIM + 1]).astype(BF16)


def _flash(qt, kc, vct, kx, vxt, *, heads):
    bsz, _, n = qt.shape
    nc = kc.shape[1]
    tq = min(512, n)
    tk = min(512, n // RING)
    unroll = FLASH_UNROLL if (n // tk) % FLASH_UNROLL == 0 else RING
    assert (n // tk) % unroll == 0
    return pl.pallas_call(
        functools.partial(_flash_kernel, tk=tk, unroll=unroll),
        grid=(bsz, heads, n // tq),
        in_specs=[
            pl.BlockSpec((1, HEAD_W, tq), lambda b, h, i: (b, h, i)),
            pl.BlockSpec((1, nc, HEAD_W), lambda b, h, i: (b, 0, h)),
            pl.BlockSpec((1, VT_ROWS, nc), lambda b, h, i: (b, h, 0)),
            pl.BlockSpec((1, n, HEAD_W), lambda b, h, i: (b, 0, h)),
            pl.BlockSpec((1, VT_ROWS, n), lambda b, h, i: (b, h, 0)),
        ],
        out_specs=pl.BlockSpec((1, V_DIM, tq), lambda b, h, i: (b, h, i)),
        out_shape=jax.ShapeDtypeStruct((bsz, heads * V_DIM, n), BF16),
        scratch_shapes=[pltpu.VMEM((tk, tq), F32)] * RING + [pltpu.VMEM((VT_ROWS, tq), F32)],
        compiler_params=_cparams(("parallel", "parallel", "arbitrary")),
        name="flash",
    )(qt, kc, vct, kx, vxt)


def _attn_out_kernel(x_ref, at_ref, w_ref, mod_ref, o_ref):
    y = lax.dot_general(at_ref[0], w_ref[...], (((0,), (0,)), ((), ())),
                        preferred_element_type=F32)
    o_ref[0] = x_ref[0] + mod_ref[0, 5:6, :] * y


def _attn_out(x3, at, w, mod):
    bsz, n, d = x3.shape
    din = at.shape[1]
    tm = min(512, n)
    return pl.pallas_call(
        _attn_out_kernel,
        grid=(bsz, n // tm),
        in_specs=[
            pl.BlockSpec((1, tm, d), lambda b, i: (b, i, 0)),
            pl.BlockSpec((1, din, tm), lambda b, i: (b, 0, i)),
            pl.BlockSpec((din, d), lambda b, i: (0, 0)),
            pl.BlockSpec((1, N_MOD, d), lambda b, i: (b, 0, 0)),
        ],
        out_specs=pl.BlockSpec((1, tm, d), lambda b, i: (b, i, 0)),
        out_shape=jax.ShapeDtypeStruct((bsz, n, d), F32),
        compiler_params=_cparams(("parallel", "parallel")),
        name="attn_out",
    )(x3, at, w, mod)


def _rope_rot_cols(w):
    q = QK_ROPE // 4
    parts = w.reshape(*w.shape[:-1], 2, 2, q)
    rot = jnp.stack([-parts[..., 1, :], parts[..., 0, :]], axis=-2)
    return rot.reshape(w.shape)


def _pad_to(a, size, axis):
    pad = [(0, 0)] * a.ndim
    pad[axis] = (0, size - a.shape[axis])
    return jnp.pad(a, pad)


def _mla_weights(w_dq, q_norm, w_uq, w_dkv, kv_norm, w_ukv, heads):
    d, q_lora = w_dq.shape
    kv_lora = kv_norm.shape[0]
    qk_dim = QK_NOPE + QK_ROPE
    ql_pad = -(-q_lora // LANES) * LANES
    w_dq_p = _pad_to(w_dq, ql_pad, 1).astype(BF16)
    q_norm_p = _pad_to(q_norm, ql_pad, 0).reshape(1, ql_pad)
    uq = w_uq.reshape(q_lora, heads, qk_dim)
    uq_nope, uq_pe = uq[..., :QK_NOPE], uq[..., QK_NOPE:]
    zeros = jnp.zeros((q_lora, heads, LANES - QK_ROPE), w_uq.dtype)
    w_q = jnp.concatenate([uq_pe, zeros, uq_nope], axis=-1).reshape(q_lora, heads * HEAD_W)
    w_qr = jnp.concatenate([_rope_rot_cols(uq_pe), zeros], axis=-1).reshape(q_lora, heads * LANES)
    w_q = _pad_to(w_q, ql_pad, 0).astype(BF16)
    w_qr = _pad_to(w_qr, ql_pad, 0).astype(BF16)
    w_c = _pad_to(w_dkv[:, :kv_lora], 2 * LANES, 1)
    w_pe = w_dkv[:, kv_lora:]
    w_kv = jnp.concatenate([w_c, _pad_to(w_pe, LANES, 1), _pad_to(_rope_rot_cols(w_pe), LANES, 1)],
                           axis=1).astype(BF16)
    kv_norm_p = _pad_to(kv_norm, 2 * LANES, 0).reshape(1, 2 * LANES)
    ukv = w_ukv.reshape(kv_lora, heads, QK_NOPE + V_DIM)
    w_ukv_p = jnp.concatenate([ukv[..., :QK_NOPE].reshape(kv_lora, heads * QK_NOPE),
                               ukv[..., QK_NOPE:].reshape(kv_lora, heads * V_DIM)], axis=1)
    w_ukv_p = _pad_to(w_ukv_p, 2 * LANES, 0).astype(BF16)
    nope_w = heads * QK_NOPE
    w_uk, w_uvt = w_ukv_p[:, :nope_w], w_ukv_p[:, nope_w:].T
    return (w_dq_p, q_norm_p, w_q.T, w_qr.T, w_kv, kv_norm_p, w_uk, w_uvt), q_lora, kv_lora


def _rope_tables(n):
    t = jnp.arange(n)
    row = (t // GRID_W).astype(F32)
    col = (t % GRID_W).astype(F32)
    half = QK_ROPE // 4
    inv = ROPE_THETA ** (-jnp.arange(half, dtype=F32) / half)
    ang_r = row[:, None] * inv
    ang_c = col[:, None] * inv
    ang = jnp.concatenate([ang_r, ang_r, ang_c, ang_c], axis=-1)
    cos = _pad_to(jnp.cos(ang), LANES, 1)
    sin = _pad_to(jnp.sin(ang), LANES, 1)
    return cos, sin


def kernel(x, c, ctx, c_ctx, mod_w, mod_b, norm_g, ffn_w_gate, ffn_w_up, ffn_w_down,
           gm_w_in, gm_ln_g, gm_ln_b, gm_w_s, gm_b_s, gm_w_out,
           mla_w_dq, mla_q_norm, mla_w_uq, mla_w_dkv, mla_kv_norm, mla_w_ukv, mla_w_o,
           final_g):
    bsz, n, d = x.shape
    n_ctx = ctx.shape[1]
    depth = mod_w.shape[0]
    heads = d // V_DIM
    assert bsz + 1 <= 8

    cond8 = jnp.concatenate([c, c_ctx[None], jnp.zeros((8 - bsz - 1, d), F32)], axis=0)
    mods = _adaln(cond8, mod_w, mod_b).reshape(depth, 8, N_MOD, d)

    xt = x.reshape(bsz * n, d)
    yt = ctx.reshape(bsz * n_ctx, d)

    def ffn_w(i, k):
        return (ffn_w_gate[i, k].astype(BF16), ffn_w_up[i, k].astype(BF16),
                ffn_w_down[i, k].astype(BF16))

    for i in range(depth):
        last = i == depth - 1
        use_a = i % 2 == 0
        j = i // 2
        m_x = mods[i, :bsz]
        m_y = mods[i, bsz:bsz + 1]
        ctx_needed = (not last) or (not use_a)

        w_pre, w_post = ffn_w(i, 0), ffn_w(i, 1)
        xt = _ffn(xt, m_x, 0, norm_g[i, 0], *w_pre)
        if ctx_needed:
            yt = _ffn(yt, m_y, 0, norm_g[i, 0], *w_pre)

        if use_a:
            inner = gm_w_out.shape[1]
            w_u = gm_w_in[j, :, :inner].astype(BF16)
            w_v = gm_w_in[j, :, inner:].astype(BF16)
            w_s = gm_w_s[j].astype(BF16)
            b_full = jnp.repeat(gm_b_s[j].T, inner // GM_GROUPS, axis=1)
            w_out = gm_w_out[j].astype(BF16)
            gm = (norm_g[i, 1], w_u, w_v, gm_ln_g[j], gm_ln_b[j], w_s, b_full)
            xt = _proj_res(xt, _gmlp_gate(xt, m_x, *gm), w_out, m_x)
            if not last:
                yt = _proj_res(yt, _gmlp_gate(yt, m_y, *gm), w_out, m_y)
        else:
            wts, q_lora, kv_lora = _mla_weights(mla_w_dq[j], mla_q_norm[j], mla_w_uq[j],
                                                mla_w_dkv[j], mla_kv_norm[j], mla_w_ukv[j], heads)
            q_scale = (QK_NOPE + QK_ROPE) ** -0.5 * math.log2(math.e)
            cos, sin = _rope_tables(n)
            ones = _pad_to(jnp.ones((n_ctx, QK_ROPE), F32), LANES, 1)
            zeros = jnp.zeros((n_ctx, LANES), F32)
            kw = dict(heads=heads, q_lora=q_lora, kv_lora=kv_lora, q_scale=q_scale)
            kc, vct = _mla_proj(yt.reshape(bsz, n_ctx, d), m_y, norm_g[i, 1], ones, zeros,
                                (None,) * 4 + wts[4:], with_q=False, **kw)
            qt, kx, vxt = _mla_proj(xt.reshape(bsz, n, d), m_x, norm_g[i, 1], cos, sin, wts,
                                    with_q=True, **kw)
            att_t = _flash(qt, kc, vct, kx, vxt, heads=heads)
            xt = _attn_out(xt.reshape(bsz, n, d), att_t, mla_w_o[j].astype(BF16), m_x)
            xt = xt.reshape(bsz * n, d)
            assert last, "context attention output is only needed for non-final MLA layers"

        xt = _ffn(xt, m_x, 2, norm_g[i, 2], *w_post, final_g=final_g if last else None)
        if not last:
            yt = _ffn(yt, m_y, 2, norm_g[i, 2], *w_post)

    return xt.reshape(bsz, n, d)
```

```python
import functools
import math

import jax
import jax.numpy as jnp
from jax import lax
from jax.experimental import pallas as pl
from jax.experimental.pallas import tpu as pltpu

EPS = 1e-6
N_MOD = 9
GRID_W = 64
CHUNK = 128
GM_GROUPS = 8
QK_NOPE = 128
QK_ROPE = 64
V_DIM = 128
ROPE_THETA = 10000.0
HEAD_W = 256
LANES = 128
SUM_ROWS = 16
VT_ROWS = V_DIM + SUM_ROWS
SUBLANES = 8
MAX_CHAINS = 4
RING = 4
LOOKAHEAD = 2
FLASH_UNROLL = 16
ROW_CHUNK = 16
ROW_UNROLL = 8
VMEM_LIMIT = 56 * 1024 * 1024
VMEM_LIMIT_FFN = 60 * 1024 * 1024

BF16 = jnp.bfloat16
F32 = jnp.float32


def _cparams(sem, vmem=VMEM_LIMIT, flags=None):
    return pltpu.CompilerParams(dimension_semantics=sem, vmem_limit_bytes=vmem, flags=flags)


def _rms(x, width=None):
    n = x.shape[-1] if width is None else width
    return x * lax.rsqrt(jnp.sum(x * x, axis=-1, keepdims=True) * (1.0 / n) + EPS)


def _premod(x, mod_ref, k, g):
    y = _rms(x) * g
    return y * (1.0 + mod_ref[0, 3 * k + 1:3 * k + 2, :]) + mod_ref[0, 3 * k:3 * k + 1, :]


def _adaln_kernel(c_ref, w_ref, b_ref, o_ref):
    c = c_ref[...]
    s = c * jax.nn.sigmoid(c)
    o_ref[0] = jnp.dot(s, w_ref[0], preferred_element_type=F32,
                       precision=lax.Precision.HIGHEST) + b_ref[0]


def _adaln(cond8, mod_w, mod_b):
    depth, d, n = mod_w.shape
    tn = min(d, 1024)
    assert n % tn == 0
    return pl.pallas_call(
        _adaln_kernel,
        grid=(depth, n // tn),
        in_specs=[
            pl.BlockSpec((8, d), lambda l, j: (0, 0)),
            pl.BlockSpec((1, d, tn), lambda l, j: (l, 0, j)),
            pl.BlockSpec((1, 1, tn), lambda l, j: (l, 0, j)),
        ],
        out_specs=pl.BlockSpec((1, 8, tn), lambda l, j: (l, 0, j)),
        out_shape=jax.ShapeDtypeStruct((depth, 8, n), F32),
        compiler_params=_cparams(("parallel", "parallel")),
        name="adaln",
    )(cond8, mod_w, mod_b.reshape(depth, 1, n))


def _ffn_kernel(*refs, k, final):
    if final:
        x_ref, mod_ref, g_ref, wg_ref, wu_ref, wd_ref, fg_ref, o_ref, h_ref, gs_ref, sh_ref = refs
    else:
        x_ref, mod_ref, g_ref, wg_ref, wu_ref, wd_ref, o_ref, h_ref, gs_ref, sh_ref = refs
    f = pl.program_id(1)

    n_row_chunks = o_ref.shape[0] // ROW_CHUNK

    def row_chunk(r):
        return pl.ds(pl.multiple_of(r * ROW_CHUNK, ROW_CHUNK), ROW_CHUNK)

    @pl.when(f == 0)
    def _():
        gain = g_ref[...] * (1.0 + mod_ref[0, 3 * k + 1:3 * k + 2, :])
        gs_ref[...] = jnp.broadcast_to(gain, gs_ref.shape)
        sh_ref[...] = jnp.broadcast_to(mod_ref[0, 3 * k:3 * k + 1, :], sh_ref.shape)

        def chunk(r, carry):
            x = x_ref[row_chunk(r), :]
            h_ref[row_chunk(r), :] = (_rms(x) * gs_ref[...] + sh_ref[...]).astype(BF16)
            o_ref[row_chunk(r), :] = x
            return carry
        lax.fori_loop(0, n_row_chunks, chunk, 0, unroll=ROW_UNROLL)

    h = h_ref[...]
    a = jnp.dot(h, wg_ref[...], preferred_element_type=F32)
    b = jnp.dot(h, wu_ref[...], preferred_element_type=F32)
    p = (a * jax.nn.sigmoid(a) * b).astype(BF16)
    half_gate = 0.5 * mod_ref[0, 3 * k + 2:3 * k + 3, :]
    o_ref[...] += jnp.dot(p, wd_ref[...], preferred_element_type=F32) * half_gate

    if final:
        @pl.when(f == pl.num_programs(1) - 1)
        def _():
            o_ref[...] = _rms(o_ref[...]) * fg_ref[...]


def _ffn(x, mod, k, g, wg, wu, wd, layer, which, final_g=None):
    t, d = x.shape
    nb = mod.shape[0]
    ff = wg.shape[-1]
    tm = min(1024, t // nb)
    tf = min(512, ff)
    tiles_per_mod = (t // nb) // tm
    final = final_g is not None
    in_specs = [
        pl.BlockSpec((tm, d), lambda i, f: (i, 0)),
        pl.BlockSpec((1, N_MOD, d), lambda i, f: (i // tiles_per_mod, 0, 0)),
        pl.BlockSpec((1, d), lambda i, f: (0, 0)),
        pl.BlockSpec((None, None, d, tf), lambda i, f: (layer, which, 0, f)),
        pl.BlockSpec((None, None, d, tf), lambda i, f: (layer, which, 0, f)),
        pl.BlockSpec((None, None, tf, d), lambda i, f: (layer, which, f, 0)),
    ]
    args = [x, mod, g.reshape(1, d), wg, wu, wd]
    if final:
        in_specs.append(pl.BlockSpec((1, d), lambda i, f: (0, 0)))
        args.append(final_g.reshape(1, d))
    return pl.pallas_call(
        functools.partial(_ffn_kernel, k=k, final=final),
        grid=(t // tm, ff // tf),
        in_specs=in_specs,
        out_specs=pl.BlockSpec((tm, d), lambda i, f: (i, 0)),
        out_shape=jax.ShapeDtypeStruct((t, d), F32),
        scratch_shapes=[pltpu.VMEM((tm, d), BF16), pltpu.VMEM((ROW_CHUNK, d), F32),
                        pltpu.VMEM((ROW_CHUNK, d), F32)],
        compiler_params=_cparams(("parallel", "arbitrary"), VMEM_LIMIT_FFN),
        name="ffn",
    )(*args)


def _proj_res_kernel(x_ref, a_ref, w_ref, mod_ref, o_ref):
    y = jnp.dot(a_ref[...], w_ref[...], preferred_element_type=F32)
    o_ref[...] = x_ref[...] + mod_ref[0, 5:6, :] * y


def _proj_res(x, a, w, mod):
    t, d = x.shape
    nb = mod.shape[0]
    din = a.shape[1]
    tm = min(512, t // nb)
    tiles_per_mod = (t // nb) // tm
    return pl.pallas_call(
        _proj_res_kernel,
        grid=(t // tm,),
        in_specs=[
            pl.BlockSpec((tm, d), lambda i: (i, 0)),
            pl.BlockSpec((tm, din), lambda i: (i, 0)),
            pl.BlockSpec((din, d), lambda i: (0, 0)),
            pl.BlockSpec((1, N_MOD, d), lambda i: (i // tiles_per_mod, 0, 0)),
        ],
        out_specs=pl.BlockSpec((tm, d), lambda i: (i, 0)),
        out_shape=jax.ShapeDtypeStruct((t, d), F32),
        compiler_params=_cparams(("parallel",)),
        name="proj_res",
    )(x, a, w, mod)


def _gmlp_kernel(x_ref, mod_ref, g_ref, wu_ref, wv_ref, lng_ref, lnb_ref, ws_ref,
                 bs_ref, o_ref, v_ref):
    tm, inner = o_ref.shape
    gdim = inner // GM_GROUPS
    h = _premod(x_ref[...], mod_ref, 1, g_ref[...]).astype(BF16)
    v = jax.nn.gelu(jnp.dot(h, wv_ref[...], preferred_element_type=F32))
    mu = jnp.mean(v, axis=-1, keepdims=True)
    vc = v - mu
    var = jnp.mean(vc * vc, axis=-1, keepdims=True)
    v_ref[...] = ((vc * lax.rsqrt(var + EPS)) * lng_ref[...] + lnb_ref[...]).astype(BF16)
    for c in range(tm // CHUNK):
        rows = slice(c * CHUNK, (c + 1) * CHUNK)
        for gi in range(GM_GROUPS):
            cols = slice(gi * gdim, (gi + 1) * gdim)
            mixed = jnp.dot(ws_ref[gi], v_ref[rows, cols], preferred_element_type=F32)
            o_ref[rows, cols] = (mixed + bs_ref[:, cols]).astype(BF16)
    u = jax.nn.gelu(jnp.dot(h, wu_ref[...], preferred_element_type=F32))
    o_ref[...] = (u * o_ref[...].astype(F32)).astype(BF16)


def _gmlp_gate(x, mod, g, w_u, w_v, ln_g, ln_b, w_s, b_s_full):
    t, d = x.shape
    nb = mod.shape[0]
    inner = w_u.shape[1]
    tm = min(256, t // nb)
    tiles_per_mod = (t // nb) // tm
    const2 = lambda i: (0, 0)
    return pl.pallas_call(
        _gmlp_kernel,
        grid=(t // tm,),
        in_specs=[
            pl.BlockSpec((tm, d), lambda i: (i, 0)),
            pl.BlockSpec((1, N_MOD, d), lambda i: (i // tiles_per_mod, 0, 0)),
            pl.BlockSpec((1, d), const2),
            pl.BlockSpec((d, inner), const2),
            pl.BlockSpec((d, inner), const2),
            pl.BlockSpec((1, inner), const2),
            pl.BlockSpec((1, inner), const2),
            pl.BlockSpec((GM_GROUPS, CHUNK, CHUNK), lambda i: (0, 0, 0)),
            pl.BlockSpec((CHUNK, inner), const2),
        ],
        out_specs=pl.BlockSpec((tm, inner), lambda i: (i, 0)),
        out_shape=jax.ShapeDtypeStruct((t, inner), BF16),
        scratch_shapes=[pltpu.VMEM((tm, inner), BF16)],
        compiler_params=_cparams(("parallel",)),
        name="gmlp_gate",
    )(x, mod, g.reshape(1, d), w_u, w_v, ln_g.reshape(1, inner), ln_b.reshape(1, inner),
      w_s, b_s_full)


def _mla_proj_kernel(*refs, heads, q_lora, kv_lora, q_scale, with_q):
    if with_q:
        (x_ref, mod_ref, g_ref, cos_ref, sin_ref, cost_ref, sint_ref, wdq_ref, qn_ref, wqt_ref,
         wqrt_ref, wkv_ref, kvn_ref, wuk_ref, wuvt_ref, qt_ref, k_ref, vt_ref) = refs
    else:
        (x_ref, mod_ref, g_ref, cos_ref, sin_ref,
         wkv_ref, kvn_ref, wuk_ref, wuvt_ref, k_ref, vt_ref) = refs
    h = _premod(x_ref[0], mod_ref, 1, g_ref[...]).astype(BF16)

    kv = jnp.dot(h, wkv_ref[...], preferred_element_type=F32)
    c_kv = _rms(kv[:, :2 * LANES], kv_lora) * kvn_ref[...]
    k_pe = (kv[:, 2 * LANES:3 * LANES] * cos_ref[...] + kv[:, 3 * LANES:] * sin_ref[...]).astype(BF16)
    up_k = jnp.dot(c_kv.astype(BF16), wuk_ref[...], preferred_element_type=F32)
    for hd in range(heads):
        k_ref[0, :, hd * HEAD_W:hd * HEAD_W + LANES] = k_pe
        k_ref[0, :, hd * HEAD_W + LANES:(hd + 1) * HEAD_W] = (
            up_k[:, hd * QK_NOPE:(hd + 1) * QK_NOPE].astype(BF16))
    v_t = jnp.dot(wuvt_ref[...], c_kv.T.astype(BF16), preferred_element_type=F32).astype(BF16)
    ones = jnp.ones((SUM_ROWS, v_t.shape[1]), BF16)
    for hd in range(heads):
        vt_ref[0, hd * VT_ROWS:hd * VT_ROWS + V_DIM, :] = v_t[hd * V_DIM:(hd + 1) * V_DIM]
        vt_ref[0, hd * VT_ROWS + V_DIM:(hd + 1) * VT_ROWS, :] = ones

    if with_q:
        ql = jnp.dot(h, wdq_ref[...], preferred_element_type=F32)
        qn_t = (_rms(ql, q_lora) * qn_ref[...] * q_scale).T.astype(BF16)
        qa_t = jnp.dot(wqt_ref[...], qn_t, preferred_element_type=F32)
        qr_t = jnp.dot(wqrt_ref[...], qn_t, preferred_element_type=F32)
        cos_t = cost_ref[...]
        sin_t = sint_ref[...]
        for hd in range(heads):
            pe = (qa_t[hd * HEAD_W:hd * HEAD_W + LANES] * cos_t
                  + qr_t[hd * LANES:(hd + 1) * LANES] * sin_t)
            qt_ref[0, hd * HEAD_W:hd * HEAD_W + LANES, :] = pe.astype(BF16)
            qt_ref[0, hd * HEAD_W + LANES:(hd + 1) * HEAD_W, :] = (
                qa_t[hd * HEAD_W + LANES:(hd + 1) * HEAD_W].astype(BF16))


def _mla_proj(x3, mod, g, cos, sin, wts, *, heads, q_lora, kv_lora, q_scale, with_q):
    bsz, n, d = x3.shape
    nb = mod.shape[0]
    tm = min(256, n)
    mod_idx = (lambda b, i: (b, 0, 0)) if nb == bsz else (lambda b, i: (0, 0, 0))
    const2 = lambda b, i: (0, 0)
    w_dq, q_norm, w_qt, w_qrt, w_kv, kv_norm, w_uk, w_uvt = wts
    in_specs = [
        pl.BlockSpec((1, tm, d), lambda b, i: (b, i, 0)),
        pl.BlockSpec((1, N_MOD, d), mod_idx),
        pl.BlockSpec((1, d), const2),
        pl.BlockSpec((tm, LANES), lambda b, i: (i, 0)),
        pl.BlockSpec((tm, LANES), lambda b, i: (i, 0)),
    ]
    args = [x3, mod, g.reshape(1, d), cos, sin]
    if with_q:
        in_specs += [pl.BlockSpec((LANES, tm), lambda b, i: (0, i)),
                     pl.BlockSpec((LANES, tm), lambda b, i: (0, i))]
        args += [cos.T, sin.T]
        in_specs += [pl.BlockSpec(w.shape, const2) for w in (w_dq, q_norm, w_qt, w_qrt)]
        args += [w_dq, q_norm, w_qt, w_qrt]
    in_specs += [pl.BlockSpec(w.shape, const2) for w in (w_kv, kv_norm, w_uk, w_uvt)]
    args += [w_kv, kv_norm, w_uk, w_uvt]
    out_specs = [pl.BlockSpec((1, tm, heads * HEAD_W), lambda b, i: (b, i, 0)),
                 pl.BlockSpec((1, heads * VT_ROWS, tm), lambda b, i: (b, 0, i))]
    out_shape = [jax.ShapeDtypeStruct((bsz, n, heads * HEAD_W), BF16),
                 jax.ShapeDtypeStruct((bsz, heads * VT_ROWS, n), BF16)]
    if with_q:
        out_specs = [pl.BlockSpec((1, heads * HEAD_W, tm), lambda b, i: (b, 0, i))] + out_specs
        out_shape = [jax.ShapeDtypeStruct((bsz, heads * HEAD_W, n), BF16)] + out_shape
    return pl.pallas_call(
        functools.partial(_mla_proj_kernel, heads=heads, q_lora=q_lora, kv_lora=kv_lora,
                          q_scale=q_scale, with_q=with_q),
        grid=(bsz, n // tm),
        in_specs=in_specs,
        out_specs=out_specs,
        out_shape=out_shape,
        compiler_params=_cparams(("parallel", "parallel")),
        name="mla_proj_q" if with_q else "mla_proj_kv",
    )(*args)


def _col_max(s):
    rows = s.shape[0]
    parts = [s[r * SUBLANES:(r + 1) * SUBLANES] for r in range(MAX_CHAINS)]
    for r in range(MAX_CHAINS, rows // SUBLANES):
        parts[r % MAX_CHAINS] = jnp.maximum(parts[r % MAX_CHAINS], s[r * SUBLANES:(r + 1) * SUBLANES])
    while len(parts) > 1:
        parts = [jnp.maximum(parts[i], parts[i + 1]) for i in range(0, len(parts), 2)]
    return jnp.max(parts[0], axis=0, keepdims=True)


def _flash_kernel(qt_ref, kc_ref, vct_ref, kx_ref, vxt_ref, o_ref, *scratch, tk, unroll):
    s_bufs, acc_ref = scratch[:RING], scratch[RING]
    qt = qt_ref[0]
    n_chunks = kx_ref.shape[1] // tk

    def pv(vt, p):
        return jnp.dot(vt, p, preferred_element_type=F32)

    def probs(s, m):
        return jnp.exp2((s - m).astype(BF16))

    s = jnp.dot(kc_ref[0], qt, preferred_element_type=F32)
    m = _col_max(s)
    acc_ref[...] = pv(vct_ref[0], probs(s, m))
    cmaxes = []
    for c in range(LOOKAHEAD):
        s = jnp.dot(kx_ref[0, c * tk:(c + 1) * tk, :], qt, preferred_element_type=F32)
        s_bufs[c][...] = s
        cmaxes.append(_col_max(s))

    def step(j, u, carry):
        m, cmax, *cmax_ahead = carry
        cur = pl.multiple_of(j * tk, tk)
        nxt = pl.multiple_of(jnp.minimum(j + LOOKAHEAD, n_chunks - 1) * tk, tk)
        m_new = jnp.maximum(m, cmax)
        p = probs(s_bufs[u][...], m_new)
        acc_ref[...] = jnp.exp2(m - m_new) * acc_ref[...] + pv(vxt_ref[0, :, pl.ds(cur, tk)], p)
        s_next = jnp.dot(kx_ref[0, pl.ds(nxt, tk), :], qt, preferred_element_type=F32)
        s_bufs[(u + LOOKAHEAD) % RING][...] = s_next
        return (m_new, *cmax_ahead, _col_max(s_next))

    def body(i, carry):
        for u in range(unroll):
            carry = step(unroll * i + u, u % RING, carry)
        return carry

    lax.fori_loop(0, n_chunks // unroll, body, (m, *cmaxes))
    acc = acc_ref[...]
    o_ref[0] = (acc[:V_DIM] / acc[V_DIM:V_DIM + 1]).astype(BF16)


def _flash(qt, kc, vct, kx, vxt, *, heads):
    bsz, _, n = qt.shape
    nc = kc.shape[1]
    tq = min(512, n)
    tk = min(512, n // RING)
    unroll = FLASH_UNROLL if (n // tk) % FLASH_UNROLL == 0 else RING
    assert (n // tk) % unroll == 0
    return pl.pallas_call(
        functools.partial(_flash_kernel, tk=tk, unroll=unroll),
        grid=(bsz, heads, n // tq),
        in_specs=[
            pl.BlockSpec((1, HEAD_W, tq), lambda b, h, i: (b, h, i)),
            pl.BlockSpec((1, nc, HEAD_W), lambda b, h, i: (b, 0, h)),
            pl.BlockSpec((1, VT_ROWS, nc), lambda b, h, i: (b, h, 0)),
            pl.BlockSpec((1, n, HEAD_W), lambda b, h, i: (b, 0, h)),
            pl.BlockSpec((1, VT_ROWS, n), lambda b, h, i: (b, h, 0)),
        ],
        out_specs=pl.BlockSpec((1, V_DIM, tq), lambda b, h, i: (b, h, i)),
        out_shape=jax.ShapeDtypeStruct((bsz, heads * V_DIM, n), BF16),
        scratch_shapes=[pltpu.VMEM((tk, tq), F32)] * RING + [pltpu.VMEM((VT_ROWS, tq), F32)],
        compiler_params=_cparams(("parallel", "parallel", "arbitrary")),
        name="flash",
    )(qt, kc, vct, kx, vxt)


def _attn_out_kernel(x_ref, at_ref, w_ref, mod_ref, o_ref):
    y = lax.dot_general(at_ref[0], w_ref[...], (((0,), (0,)), ((), ())),
                        preferred_element_type=F32)
    o_ref[0] = x_ref[0] + mod_ref[0, 5:6, :] * y


def _attn_out(x3, at, w, mod):
    bsz, n, d = x3.shape
    din = at.shape[1]
    tm = min(512, n)
    return pl.pallas_call(
        _attn_out_kernel,
        grid=(bsz, n // tm),
        in_specs=[
            pl.BlockSpec((1, tm, d), lambda b, i: (b, i, 0)),
            pl.BlockSpec((1, din, tm), lambda b, i: (b, 0, i)),
            pl.BlockSpec((din, d), lambda b, i: (0, 0)),
            pl.BlockSpec((1, N_MOD, d), lambda b, i: (b, 0, 0)),
        ],
        out_specs=pl.BlockSpec((1, tm, d), lambda b, i: (b, i, 0)),
        out_shape=jax.ShapeDtypeStruct((bsz, n, d), F32),
        compiler_params=_cparams(("parallel", "parallel")),
        name="attn_out",
    )(x3, at, w, mod)


def _rope_rot_cols(w):
    q = QK_ROPE // 4
    parts = w.reshape(*w.shape[:-1], 2, 2, q)
    rot = jnp.stack([-parts[..., 1, :], parts[..., 0, :]], axis=-2)
    return rot.reshape(w.shape)


def _pad_to(a, size, axis):
    pad = [(0, 0)] * a.ndim
    pad[axis] = (0, size - a.shape[axis])
    return jnp.pad(a, pad)


def _mla_weights(w_dq, q_norm, w_uq, w_dkv, kv_norm, w_ukv, heads):
    d, q_lora = w_dq.shape
    kv_lora = kv_norm.shape[0]
    qk_dim = QK_NOPE + QK_ROPE
    ql_pad = -(-q_lora // LANES) * LANES
    w_dq_p = _pad_to(w_dq, ql_pad, 1).astype(BF16)
    q_norm_p = _pad_to(q_norm, ql_pad, 0).reshape(1, ql_pad)
    uq = w_uq.reshape(q_lora, heads, qk_dim)
    uq_nope, uq_pe = uq[..., :QK_NOPE], uq[..., QK_NOPE:]
    zeros = jnp.zeros((q_lora, heads, LANES - QK_ROPE), w_uq.dtype)
    w_q = jnp.concatenate([uq_pe, zeros, uq_nope], axis=-1).reshape(q_lora, heads * HEAD_W)
    w_qr = jnp.concatenate([_rope_rot_cols(uq_pe), zeros], axis=-1).reshape(q_lora, heads * LANES)
    w_q = _pad_to(w_q, ql_pad, 0).astype(BF16)
    w_qr = _pad_to(w_qr, ql_pad, 0).astype(BF16)
    w_c = _pad_to(w_dkv[:, :kv_lora], 2 * LANES, 1)
    w_pe = w_dkv[:, kv_lora:]
    w_kv = jnp.concatenate([w_c, _pad_to(w_pe, LANES, 1), _pad_to(_rope_rot_cols(w_pe), LANES, 1)],
                           axis=1).astype(BF16)
    kv_norm_p = _pad_to(kv_norm, 2 * LANES, 0).reshape(1, 2 * LANES)
    ukv = w_ukv.reshape(kv_lora, heads, QK_NOPE + V_DIM)
    w_ukv_p = jnp.concatenate([ukv[..., :QK_NOPE].reshape(kv_lora, heads * QK_NOPE),
                               ukv[..., QK_NOPE:].reshape(kv_lora, heads * V_DIM)], axis=1)
    w_ukv_p = _pad_to(w_ukv_p, 2 * LANES, 0).astype(BF16)
    nope_w = heads * QK_NOPE
    w_uk, w_uvt = w_ukv_p[:, :nope_w], w_ukv_p[:, nope_w:].T
    return (w_dq_p, q_norm_p, w_q.T, w_qr.T, w_kv, kv_norm_p, w_uk, w_uvt), q_lora, kv_lora


def _rope_tables(n):
    t = jnp.arange(n)
    row = (t // GRID_W).astype(F32)
    col = (t % GRID_W).astype(F32)
    half = QK_ROPE // 4
    inv = ROPE_THETA ** (-jnp.arange(half, dtype=F32) / half)
    ang_r = row[:, None] * inv
    ang_c = col[:, None] * inv
    ang = jnp.concatenate([ang_r, ang_r, ang_c, ang_c], axis=-1)
    cos = _pad_to(jnp.cos(ang), LANES, 1)
    sin = _pad_to(jnp.sin(ang), LANES, 1)
    return cos, sin


def kernel(x, c, ctx, c_ctx, mod_w, mod_b, norm_g, ffn_w_gate, ffn_w_up, ffn_w_down,
           gm_w_in, gm_ln_g, gm_ln_b, gm_w_s, gm_b_s, gm_w_out,
           mla_w_dq, mla_q_norm, mla_w_uq, mla_w_dkv, mla_kv_norm, mla_w_ukv, mla_w_o,
           final_g):
    bsz, n, d = x.shape
    n_ctx = ctx.shape[1]
    depth = mod_w.shape[0]
    heads = d // V_DIM
    assert bsz + 1 <= 8

    cond8 = jnp.concatenate([c, c_ctx[None], jnp.zeros((8 - bsz - 1, d), F32)], axis=0)
    mods = _adaln(cond8, mod_w, mod_b).reshape(depth, 8, N_MOD, d)

    xt = x.reshape(bsz * n, d)
    yt = ctx.reshape(bsz * n_ctx, d)
    ffn_w = (ffn_w_gate.astype(BF16), ffn_w_up.astype(BF16), ffn_w_down.astype(BF16))

    for i in range(depth):
        last = i == depth - 1
        use_a = i % 2 == 0
        j = i // 2
        m_x = mods[i, :bsz]
        m_y = mods[i, bsz:bsz + 1]
        ctx_needed = (not last) or (not use_a)

        xt = _ffn(xt, m_x, 0, norm_g[i, 0], *ffn_w, i, 0)
        if ctx_needed:
            yt = _ffn(yt, m_y, 0, norm_g[i, 0], *ffn_w, i, 0)

        if use_a:
            inner = gm_w_out.shape[1]
            w_u = gm_w_in[j, :, :inner].astype(BF16)
            w_v = gm_w_in[j, :, inner:].astype(BF16)
            w_s = gm_w_s[j].astype(BF16)
            b_full = jnp.repeat(gm_b_s[j].T, inner // GM_GROUPS, axis=1)
            w_out = gm_w_out[j].astype(BF16)
            gm = (norm_g[i, 1], w_u, w_v, gm_ln_g[j], gm_ln_b[j], w_s, b_full)
            xt = _proj_res(xt, _gmlp_gate(xt, m_x, *gm), w_out, m_x)
            if not last:
                yt = _proj_res(yt, _gmlp_gate(yt, m_y, *gm), w_out, m_y)
        else:
            wts, q_lora, kv_lora = _mla_weights(mla_w_dq[j], mla_q_norm[j], mla_w_uq[j],
                                                mla_w_dkv[j], mla_kv_norm[j], mla_w_ukv[j], heads)
            q_scale = (QK_NOPE + QK_ROPE) ** -0.5 * math.log2(math.e)
            cos, sin = _rope_tables(n)
            ones = _pad_to(jnp.ones((n_ctx, QK_ROPE), F32), LANES, 1)
            zeros = jnp.zeros((n_ctx, LANES), F32)
            kw = dict(heads=heads, q_lora=q_lora, kv_lora=kv_lora, q_scale=q_scale)
            kc, vct = _mla_proj(yt.reshape(bsz, n_ctx, d), m_y, norm_g[i, 1], ones, zeros,
                                (None,) * 4 + wts[4:], with_q=False, **kw)
            qt, kx, vxt = _mla_proj(xt.reshape(bsz, n, d), m_x, norm_g[i, 1], cos, sin, wts,
                                    with_q=True, **kw)
            att_t = _flash(qt, kc, vct, kx, vxt, heads=heads)
            xt = _attn_out(xt.reshape(bsz, n, d), att_t, mla_w_o[j].astype(BF16), m_x)
            xt = xt.reshape(bsz * n, d)
            assert last, "context attention output is only needed for non-final MLA layers"

        xt = _ffn(xt, m_x, 2, norm_g[i, 2], *ffn_w, i, 1, final_g=final_g if last else None)
        if not last:
            yt = _ffn(yt, m_y, 2, norm_g[i, 2], *ffn_w, i, 1)

    return xt.reshape(bsz, n, d)
```

```python
import functools
import math

import jax
import jax.numpy as jnp
from jax import lax
from jax.experimental import pallas as pl
from jax.experimental.pallas import tpu as pltpu

EPS = 1e-6
N_MOD = 9
GRID_W = 64
CHUNK = 128
GM_GROUPS = 8
QK_NOPE = 128
QK_ROPE = 64
V_DIM = 128
ROPE_THETA = 10000.0
HEAD_W = 256
LANES = 128
SUM_ROWS = 16
VT_ROWS = V_DIM + SUM_ROWS
SUBLANES = 8
MAX_CHAINS = 4
RING = 4
LOOKAHEAD = 2
FLASH_UNROLL = 16
ROW_CHUNK = 16
ROW_UNROLL = 8
VMEM_LIMIT = 56 * 1024 * 1024
VMEM_LIMIT_FFN = 60 * 1024 * 1024

BF16 = jnp.bfloat16
F32 = jnp.float32


def _cparams(sem, vmem=VMEM_LIMIT, flags=None):
    return pltpu.CompilerParams(dimension_semantics=sem, vmem_limit_bytes=vmem, flags=flags)


def _rms(x, width=None):
    n = x.shape[-1] if width is None else width
    return x * lax.rsqrt(jnp.sum(x * x, axis=-1, keepdims=True) * (1.0 / n) + EPS)


def _premod(x, mod_ref, k, g):
    y = _rms(x) * g
    return y * (1.0 + mod_ref[0, 3 * k + 1:3 * k + 2, :]) + mod_ref[0, 3 * k:3 * k + 1, :]


def _adaln_kernel(c_ref, w_ref, b_ref, o_ref):
    c = c_ref[...]
    s = c * jax.nn.sigmoid(c)
    o_ref[0] = jnp.dot(s, w_ref[0], preferred_element_type=F32,
                       precision=lax.Precision.HIGHEST) + b_ref[0]


def _adaln(cond8, mod_w, mod_b):
    depth, d, n = mod_w.shape
    tn = min(d, 1024)
    assert n % tn == 0
    return pl.pallas_call(
        _adaln_kernel,
        grid=(depth, n // tn),
        in_specs=[
            pl.BlockSpec((8, d), lambda l, j: (0, 0)),
            pl.BlockSpec((1, d, tn), lambda l, j: (l, 0, j)),
            pl.BlockSpec((1, 1, tn), lambda l, j: (l, 0, j)),
        ],
        out_specs=pl.BlockSpec((1, 8, tn), lambda l, j: (l, 0, j)),
        out_shape=jax.ShapeDtypeStruct((depth, 8, n), F32),
        compiler_params=_cparams(("parallel", "parallel")),
        name="adaln",
    )(cond8, mod_w, mod_b.reshape(depth, 1, n))


def _ffn_kernel(*refs, k, final):
    if final:
        x_ref, mod_ref, g_ref, wg_ref, wu_ref, wd_ref, fg_ref, o_ref, h_ref, gs_ref, sh_ref = refs
    else:
        x_ref, mod_ref, g_ref, wg_ref, wu_ref, wd_ref, o_ref, h_ref, gs_ref, sh_ref = refs
    f = pl.program_id(1)

    n_row_chunks = o_ref.shape[0] // ROW_CHUNK

    def row_chunk(r):
        return pl.ds(pl.multiple_of(r * ROW_CHUNK, ROW_CHUNK), ROW_CHUNK)

    @pl.when(f == 0)
    def _():
        gain = g_ref[...] * (1.0 + mod_ref[0, 3 * k + 1:3 * k + 2, :])
        gs_ref[...] = jnp.broadcast_to(gain, gs_ref.shape)
        sh_ref[...] = jnp.broadcast_to(mod_ref[0, 3 * k:3 * k + 1, :], sh_ref.shape)

        def chunk(r, carry):
            x = x_ref[row_chunk(r), :]
            h_ref[row_chunk(r), :] = (_rms(x) * gs_ref[...] + sh_ref[...]).astype(BF16)
            o_ref[row_chunk(r), :] = x
            return carry
        lax.fori_loop(0, n_row_chunks, chunk, 0, unroll=ROW_UNROLL)

    h = h_ref[...]
    a = jnp.dot(h, wg_ref[...], preferred_element_type=F32)
    b = jnp.dot(h, wu_ref[...], preferred_element_type=F32)
    p = (a * jax.nn.sigmoid(a) * b).astype(BF16)
    half_gate = 0.5 * mod_ref[0, 3 * k + 2:3 * k + 3, :]
    o_ref[...] += jnp.dot(p, wd_ref[...], preferred_element_type=F32) * half_gate

    if final:
        @pl.when(f == pl.num_programs(1) - 1)
        def _():
            o_ref[...] = _rms(o_ref[...]) * fg_ref[...]


def _ffn(x, mod, k, g, wg, wu, wd, layer, which, final_g=None):
    t, d = x.shape
    nb = mod.shape[0]
    ff = wg.shape[-1]
    tm = min(1024, t // nb)
    tf = min(512, ff)
    tiles_per_mod = (t // nb) // tm
    final = final_g is not None
    in_specs = [
        pl.BlockSpec((tm, d), lambda i, f: (i, 0)),
        pl.BlockSpec((1, N_MOD, d), lambda i, f: (i // tiles_per_mod, 0, 0)),
        pl.BlockSpec((1, d), lambda i, f: (0, 0)),
        pl.BlockSpec((None, None, d, tf), lambda i, f: (layer, which, 0, f)),
        pl.BlockSpec((None, None, d, tf), lambda i, f: (layer, which, 0, f)),
        pl.BlockSpec((None, None, tf, d), lambda i, f: (layer, which, f, 0)),
    ]
    args = [x, mod, g.reshape(1, d), wg, wu, wd]
    if final:
        in_specs.append(pl.BlockSpec((1, d), lambda i, f: (0, 0)))
        args.append(final_g.reshape(1, d))
    return pl.pallas_call(
        functools.partial(_ffn_kernel, k=k, final=final),
        grid=(t // tm, ff // tf),
        in_specs=in_specs,
        out_specs=pl.BlockSpec((tm, d), lambda i, f: (i, 0)),
        out_shape=jax.ShapeDtypeStruct((t, d), F32),
        scratch_shapes=[pltpu.VMEM((tm, d), BF16), pltpu.VMEM((ROW_CHUNK, d), F32),
                        pltpu.VMEM((ROW_CHUNK, d), F32)],
        compiler_params=_cparams(("parallel", "arbitrary"), VMEM_LIMIT_FFN),
        name="ffn",
    )(*args)


def _proj_res_kernel(x_ref, a_ref, w_ref, mod_ref, o_ref):
    y = jnp.dot(a_ref[...], w_ref[...], preferred_element_type=F32)
    o_ref[...] = x_ref[...] + mod_ref[0, 5:6, :] * y


def _proj_res(x, a, w, mod):
    t, d = x.shape
    nb = mod.shape[0]
    din = a.shape[1]
    tm = min(512, t // nb)
    tiles_per_mod = (t // nb) // tm
    return pl.pallas_call(
        _proj_res_kernel,
        grid=(t // tm,),
        in_specs=[
            pl.BlockSpec((tm, d), lambda i: (i, 0)),
            pl.BlockSpec((tm, din), lambda i: (i, 0)),
            pl.BlockSpec((din, d), lambda i: (0, 0)),
            pl.BlockSpec((1, N_MOD, d), lambda i: (i // tiles_per_mod, 0, 0)),
        ],
        out_specs=pl.BlockSpec((tm, d), lambda i: (i, 0)),
        out_shape=jax.ShapeDtypeStruct((t, d), F32),
        compiler_params=_cparams(("parallel",)),
        name="proj_res",
    )(x, a, w, mod)


def _gmlp_kernel(x_ref, mod_ref, g_ref, wu_ref, wv_ref, lng_ref, lnb_ref, ws_ref,
                 bs_ref, o_ref, v_ref):
    tm, inner = o_ref.shape
    gdim = inner // GM_GROUPS
    h = _premod(x_ref[...], mod_ref, 1, g_ref[...]).astype(BF16)
    v = jax.nn.gelu(jnp.dot(h, wv_ref[...], preferred_element_type=F32))
    mu = jnp.mean(v, axis=-1, keepdims=True)
    vc = v - mu
    var = jnp.mean(vc * vc, axis=-1, keepdims=True)
    v_ref[...] = ((vc * lax.rsqrt(var + EPS)) * lng_ref[...] + lnb_ref[...]).astype(BF16)
    for c in range(tm // CHUNK):
        rows = slice(c * CHUNK, (c + 1) * CHUNK)
        for gi in range(GM_GROUPS):
            cols = slice(gi * gdim, (gi + 1) * gdim)
            mixed = jnp.dot(ws_ref[gi], v_ref[rows, cols], preferred_element_type=F32)
            o_ref[rows, cols] = (mixed + bs_ref[:, cols]).astype(BF16)
    u = jax.nn.gelu(jnp.dot(h, wu_ref[...], preferred_element_type=F32))
    o_ref[...] = (u * o_ref[...].astype(F32)).astype(BF16)


def _gmlp_gate(x, mod, g, w_u, w_v, ln_g, ln_b, w_s, b_s_full):
    t, d = x.shape
    nb = mod.shape[0]
    inner = w_u.shape[1]
    tm = min(512, t // nb)
    tiles_per_mod = (t // nb) // tm
    const2 = lambda i: (0, 0)
    return pl.pallas_call(
        _gmlp_kernel,
        grid=(t // tm,),
        in_specs=[
            pl.BlockSpec((tm, d), lambda i: (i, 0)),
            pl.BlockSpec((1, N_MOD, d), lambda i: (i // tiles_per_mod, 0, 0)),
            pl.BlockSpec((1, d), const2),
            pl.BlockSpec((d, inner), const2),
            pl.BlockSpec((d, inner), const2),
            pl.BlockSpec((1, inner), const2),
            pl.BlockSpec((1, inner), const2),
            pl.BlockSpec((GM_GROUPS, CHUNK, CHUNK), lambda i: (0, 0, 0)),
            pl.BlockSpec((CHUNK, inner), const2),
        ],
        out_specs=pl.BlockSpec((tm, inner), lambda i: (i, 0)),
        out_shape=jax.ShapeDtypeStruct((t, inner), BF16),
        scratch_shapes=[pltpu.VMEM((tm, inner), BF16)],
        compiler_params=_cparams(("parallel",)),
        name="gmlp_gate",
    )(x, mod, g.reshape(1, d), w_u, w_v, ln_g.reshape(1, inner), ln_b.reshape(1, inner),
      w_s, b_s_full)


def _mla_proj_kernel(*refs, heads, q_lora, kv_lora, q_scale, with_q):
    if with_q:
        (x_ref, mod_ref, g_ref, cos_ref, sin_ref, cost_ref, sint_ref, wdq_ref, qn_ref, wqt_ref,
         wqrt_ref, wkv_ref, kvn_ref, wuk_ref, wuvt_ref, qt_ref, k_ref, vt_ref) = refs
    else:
        (x_ref, mod_ref, g_ref, cos_ref, sin_ref,
         wkv_ref, kvn_ref, wuk_ref, wuvt_ref, k_ref, vt_ref) = refs
    h = _premod(x_ref[0], mod_ref, 1, g_ref[...]).astype(BF16)

    kv = jnp.dot(h, wkv_ref[...], preferred_element_type=F32)
    c_kv = _rms(kv[:, :2 * LANES], kv_lora) * kvn_ref[...]
    k_pe = (kv[:, 2 * LANES:3 * LANES] * cos_ref[...] + kv[:, 3 * LANES:] * sin_ref[...]).astype(BF16)
    up_k = jnp.dot(c_kv.astype(BF16), wuk_ref[...], preferred_element_type=F32)
    for hd in range(heads):
        k_ref[0, :, hd * HEAD_W:hd * HEAD_W + LANES] = k_pe
        k_ref[0, :, hd * HEAD_W + LANES:(hd + 1) * HEAD_W] = (
            up_k[:, hd * QK_NOPE:(hd + 1) * QK_NOPE].astype(BF16))
    v_t = jnp.dot(wuvt_ref[...], c_kv.T.astype(BF16), preferred_element_type=F32).astype(BF16)
    ones = jnp.ones((SUM_ROWS, v_t.shape[1]), BF16)
    for hd in range(heads):
        vt_ref[0, hd * VT_ROWS:hd * VT_ROWS + V_DIM, :] = v_t[hd * V_DIM:(hd + 1) * V_DIM]
        vt_ref[0, hd * VT_ROWS + V_DIM:(hd + 1) * VT_ROWS, :] = ones

    if with_q:
        ql = jnp.dot(h, wdq_ref[...], preferred_element_type=F32)
        qn_t = (_rms(ql, q_lora) * qn_ref[...] * q_scale).T.astype(BF16)
        qa_t = jnp.dot(wqt_ref[...], qn_t, preferred_element_type=F32)
        qr_t = jnp.dot(wqrt_ref[...], qn_t, preferred_element_type=F32)
        cos_t = cost_ref[...]
        sin_t = sint_ref[...]
        for hd in range(heads):
            pe = (qa_t[hd * HEAD_W:hd * HEAD_W + LANES] * cos_t
                  + qr_t[hd * LANES:(hd + 1) * LANES] * sin_t)
            qt_ref[0, hd * HEAD_W:hd * HEAD_W + LANES, :] = pe.astype(BF16)
            qt_ref[0, hd * HEAD_W + LANES:(hd + 1) * HEAD_W, :] = (
                qa_t[hd * HEAD_W + LANES:(hd + 1) * HEAD_W].astype(BF16))


def _mla_proj(x3, mod, g, cos, sin, wts, *, heads, q_lora, kv_lora, q_scale, with_q):
    bsz, n, d = x3.shape
    nb = mod.shape[0]
    tm = min(256, n)
    mod_idx = (lambda b, i: (b, 0, 0)) if nb == bsz else (lambda b, i: (0, 0, 0))
    const2 = lambda b, i: (0, 0)
    w_dq, q_norm, w_qt, w_qrt, w_kv, kv_norm, w_uk, w_uvt = wts
    in_specs = [
        pl.BlockSpec((1, tm, d), lambda b, i: (b, i, 0)),
        pl.BlockSpec((1, N_MOD, d), mod_idx),
        pl.BlockSpec((1, d), const2),
        pl.BlockSpec((tm, LANES), lambda b, i: (i, 0)),
        pl.BlockSpec((tm, LANES), lambda b, i: (i, 0)),
    ]
    args = [x3, mod, g.reshape(1, d), cos, sin]
    if with_q:
        in_specs += [pl.BlockSpec((LANES, tm), lambda b, i: (0, i)),
                     pl.BlockSpec((LANES, tm), lambda b, i: (0, i))]
        args += [cos.T, sin.T]
        in_specs += [pl.BlockSpec(w.shape, const2) for w in (w_dq, q_norm, w_qt, w_qrt)]
        args += [w_dq, q_norm, w_qt, w_qrt]
    in_specs += [pl.BlockSpec(w.shape, const2) for w in (w_kv, kv_norm, w_uk, w_uvt)]
    args += [w_kv, kv_norm, w_uk, w_uvt]
    out_specs = [pl.BlockSpec((1, tm, heads * HEAD_W), lambda b, i: (b, i, 0)),
                 pl.BlockSpec((1, heads * VT_ROWS, tm), lambda b, i: (b, 0, i))]
    out_shape = [jax.ShapeDtypeStruct((bsz, n, heads * HEAD_W), BF16),
                 jax.ShapeDtypeStruct((bsz, heads * VT_ROWS, n), BF16)]
    if with_q:
        out_specs = [pl.BlockSpec((1, heads * HEAD_W, tm), lambda b, i: (b, 0, i))] + out_specs
        out_shape = [jax.ShapeDtypeStruct((bsz, heads * HEAD_W, n), BF16)] + out_shape
    return pl.pallas_call(
        functools.partial(_mla_proj_kernel, heads=heads, q_lora=q_lora, kv_lora=kv_lora,
                          q_scale=q_scale, with_q=with_q),
        grid=(bsz, n // tm),
        in_specs=in_specs,
        out_specs=out_specs,
        out_shape=out_shape,
        compiler_params=_cparams(("parallel", "parallel")),
        name="mla_proj_q" if with_q else "mla_proj_kv",
    )(*args)


def _col_max(s):
    rows = s.shape[0]
    parts = [s[r * SUBLANES:(r + 1) * SUBLANES] for r in range(MAX_CHAINS)]
    for r in range(MAX_CHAINS, rows // SUBLANES):
        parts[r % MAX_CHAINS] = jnp.maximum(parts[r % MAX_CHAINS], s[r * SUBLANES:(r + 1) * SUBLANES])
    while len(parts) > 1:
        parts = [jnp.maximum(parts[i], parts[i + 1]) for i in range(0, len(parts), 2)]
    return jnp.max(parts[0], axis=0, keepdims=True)


def _flash_kernel(qt_ref, kc_ref, vct_ref, kx_ref, vxt_ref, o_ref, *scratch, tk, unroll):
    s_bufs, acc_ref = scratch[:RING], scratch[RING]
    qt = qt_ref[0]
    n_chunks = kx_ref.shape[1] // tk

    def pv(vt, p):
        return jnp.dot(vt, p, preferred_element_type=F32)

    def probs(s, m):
        return jnp.exp2((s - m).astype(BF16))

    s_ctx = jnp.dot(kc_ref[0], qt, preferred_element_type=F32)
    cmaxes = []
    for c in range(LOOKAHEAD):
        s = jnp.dot(kx_ref[0, c * tk:(c + 1) * tk, :], qt, preferred_element_type=F32)
        s_bufs[c][...] = s
        cmaxes.append(_col_max(s))
    m = _col_max(s_ctx)
    acc_ref[...] = pv(vct_ref[0], probs(s_ctx, m))

    def step(j, u, carry):
        m, cmax, *cmax_ahead = carry
        cur = pl.multiple_of(j * tk, tk)
        nxt = pl.multiple_of(jnp.minimum(j + LOOKAHEAD, n_chunks - 1) * tk, tk)
        m_new = jnp.maximum(m, cmax)
        p = probs(s_bufs[u][...], m_new)
        acc_ref[...] = jnp.exp2(m - m_new) * acc_ref[...] + pv(vxt_ref[0, :, pl.ds(cur, tk)], p)
        s_next = jnp.dot(kx_ref[0, pl.ds(nxt, tk), :], qt, preferred_element_type=F32)
        s_bufs[(u + LOOKAHEAD) % RING][...] = s_next
        return (m_new, *cmax_ahead, _col_max(s_next))

    def body(i, carry):
        for u in range(unroll):
            carry = step(unroll * i + u, u % RING, carry)
        return carry

    lax.fori_loop(0, n_chunks // unroll, body, (m, *cmaxes))
    acc = acc_ref[...]
    o_ref[0] = (acc[:V_DIM] / acc[V_DIM:V_DIM + 1]).astype(BF16)


def _flash(qt, kc, vct, kx, vxt, *, heads):
    bsz, _, n = qt.shape
    nc = kc.shape[1]
    tq = min(512, n)
    tk = min(512, n // RING)
    unroll = FLASH_UNROLL if (n // tk) % FLASH_UNROLL == 0 else RING
    assert (n // tk) % unroll == 0
    return pl.pallas_call(
        functools.partial(_flash_kernel, tk=tk, unroll=unroll),
        grid=(bsz, heads, n // tq),
        in_specs=[
            pl.BlockSpec((1, HEAD_W, tq), lambda b, h, i: (b, h, i)),
            pl.BlockSpec((1, nc, HEAD_W), lambda b, h, i: (b, 0, h)),
            pl.BlockSpec((1, VT_ROWS, nc), lambda b, h, i: (b, h, 0)),
            pl.BlockSpec((1, n, HEAD_W), lambda b, h, i: (b, 0, h)),
            pl.BlockSpec((1, VT_ROWS, n), lambda b, h, i: (b, h, 0)),
        ],
        out_specs=pl.BlockSpec((1, V_DIM, tq), lambda b, h, i: (b, h, i)),
        out_shape=jax.ShapeDtypeStruct((bsz, heads * V_DIM, n), BF16),
        scratch_shapes=[pltpu.VMEM((tk, tq), F32)] * RING + [pltpu.VMEM((VT_ROWS, tq), F32)],
        compiler_params=_cparams(("parallel", "parallel", "arbitrary")),
        name="flash",
    )(qt, kc, vct, kx, vxt)


def _attn_out_kernel(x_ref, at_ref, w_ref, mod_ref, o_ref):
    y = lax.dot_general(at_ref[0], w_ref[...], (((0,), (0,)), ((), ())),
                        preferred_element_type=F32)
    o_ref[0] = x_ref[0] + mod_ref[0, 5:6, :] * y


def _attn_out(x3, at, w, mod):
    bsz, n, d = x3.shape
    din = at.shape[1]
    tm = min(512, n)
    return pl.pallas_call(
        _attn_out_kernel,
        grid=(bsz, n // tm),
        in_specs=[
            pl.BlockSpec((1, tm, d), lambda b, i: (b, i, 0)),
            pl.BlockSpec((1, din, tm), lambda b, i: (b, 0, i)),
            pl.BlockSpec((din, d), lambda b, i: (0, 0)),
            pl.BlockSpec((1, N_MOD, d), lambda b, i: (b, 0, 0)),
        ],
        out_specs=pl.BlockSpec((1, tm, d), lambda b, i: (b, i, 0)),
        out_shape=jax.ShapeDtypeStruct((bsz, n, d), F32),
        compiler_params=_cparams(("parallel", "parallel")),
        name="attn_out",
    )(x3, at, w, mod)


def _rope_rot_cols(w):
    q = QK_ROPE // 4
    parts = w.reshape(*w.shape[:-1], 2, 2, q)
    rot = jnp.stack([-parts[..., 1, :], parts[..., 0, :]], axis=-2)
    return rot.reshape(w.shape)


def _pad_to(a, size, axis):
    pad = [(0, 0)] * a.ndim
    pad[axis] = (0, size - a.shape[axis])
    return jnp.pad(a, pad)


def _mla_weights(w_dq, q_norm, w_uq, w_dkv, kv_norm, w_ukv, heads):
    d, q_lora = w_dq.shape
    kv_lora = kv_norm.shape[0]
    qk_dim = QK_NOPE + QK_ROPE
    ql_pad = -(-q_lora // LANES) * LANES
    w_dq_p = _pad_to(w_dq, ql_pad, 1).astype(BF16)
    q_norm_p = _pad_to(q_norm, ql_pad, 0).reshape(1, ql_pad)
    uq = w_uq.reshape(q_lora, heads, qk_dim)
    uq_nope, uq_pe = uq[..., :QK_NOPE], uq[..., QK_NOPE:]
    zeros = jnp.zeros((q_lora, heads, LANES - QK_ROPE), w_uq.dtype)
    w_q = jnp.concatenate([uq_pe, zeros, uq_nope], axis=-1).reshape(q_lora, heads * HEAD_W)
    w_qr = jnp.concatenate([_rope_rot_cols(uq_pe), zeros], axis=-1).reshape(q_lora, heads * LANES)
    w_q = _pad_to(w_q, ql_pad, 0).astype(BF16)
    w_qr = _pad_to(w_qr, ql_pad, 0).astype(BF16)
    w_c = _pad_to(w_dkv[:, :kv_lora], 2 * LANES, 1)
    w_pe = w_dkv[:, kv_lora:]
    w_kv = jnp.concatenate([w_c, _pad_to(w_pe, LANES, 1), _pad_to(_rope_rot_cols(w_pe), LANES, 1)],
                           axis=1).astype(BF16)
    kv_norm_p = _pad_to(kv_norm, 2 * LANES, 0).reshape(1, 2 * LANES)
    ukv = w_ukv.reshape(kv_lora, heads, QK_NOPE + V_DIM)
    w_ukv_p = jnp.concatenate([ukv[..., :QK_NOPE].reshape(kv_lora, heads * QK_NOPE),
                               ukv[..., QK_NOPE:].reshape(kv_lora, heads * V_DIM)], axis=1)
    w_ukv_p = _pad_to(w_ukv_p, 2 * LANES, 0).astype(BF16)
    nope_w = heads * QK_NOPE
    w_uk, w_uvt = w_ukv_p[:, :nope_w], w_ukv_p[:, nope_w:].T
    return (w_dq_p, q_norm_p, w_q.T, w_qr.T, w_kv, kv_norm_p, w_uk, w_uvt), q_lora, kv_lora


def _rope_tables(n):
    t = jnp.arange(n)
    row = (t // GRID_W).astype(F32)
    col = (t % GRID_W).astype(F32)
    half = QK_ROPE // 4
    inv = ROPE_THETA ** (-jnp.arange(half, dtype=F32) / half)
    ang_r = row[:, None] * inv
    ang_c = col[:, None] * inv
    ang = jnp.concatenate([ang_r, ang_r, ang_c, ang_c], axis=-1)
    cos = _pad_to(jnp.cos(ang), LANES, 1)
    sin = _pad_to(jnp.sin(ang), LANES, 1)
    return cos, sin


def kernel(x, c, ctx, c_ctx, mod_w, mod_b, norm_g, ffn_w_gate, ffn_w_up, ffn_w_down,
           gm_w_in, gm_ln_g, gm_ln_b, gm_w_s, gm_b_s, gm_w_out,
           mla_w_dq, mla_q_norm, mla_w_uq, mla_w_dkv, mla_kv_norm, mla_w_ukv, mla_w_o,
           final_g):
    bsz, n, d = x.shape
    n_ctx = ctx.shape[1]
    depth = mod_w.shape[0]
    heads = d // V_DIM
    assert bsz + 1 <= 8

    cond8 = jnp.concatenate([c, c_ctx[None], jnp.zeros((8 - bsz - 1, d), F32)], axis=0)
    mods = _adaln(cond8, mod_w, mod_b).reshape(depth, 8, N_MOD, d)

    xt = x.reshape(bsz * n, d)
    yt = ctx.reshape(bsz * n_ctx, d)
    ffn_w = (ffn_w_gate.astype(BF16), ffn_w_up.astype(BF16), ffn_w_down.astype(BF16))

    for i in range(depth):
        last = i == depth - 1
        use_a = i % 2 == 0
        j = i // 2
        m_x = mods[i, :bsz]
        m_y = mods[i, bsz:bsz + 1]
        ctx_needed = (not last) or (not use_a)

        xt = _ffn(xt, m_x, 0, norm_g[i, 0], *ffn_w, i, 0)
        if ctx_needed:
            yt = _ffn(yt, m_y, 0, norm_g[i, 0], *ffn_w, i, 0)

        if use_a:
            inner = gm_w_out.shape[1]
            w_u = gm_w_in[j, :, :inner].astype(BF16)
            w_v = gm_w_in[j, :, inner:].astype(BF16)
            w_s = gm_w_s[j].astype(BF16)
            b_full = jnp.repeat(gm_b_s[j].T, inner // GM_GROUPS, axis=1)
            w_out = gm_w_out[j].astype(BF16)
            gm = (norm_g[i, 1], w_u, w_v, gm_ln_g[j], gm_ln_b[j], w_s, b_full)
            xt = _proj_res(xt, _gmlp_gate(xt, m_x, *gm), w_out, m_x)
            if not last:
                yt = _proj_res(yt, _gmlp_gate(yt, m_y, *gm), w_out, m_y)
        else:
            wts, q_lora, kv_lora = _mla_weights(mla_w_dq[j], mla_q_norm[j], mla_w_uq[j],
                                                mla_w_dkv[j], mla_kv_norm[j], mla_w_ukv[j], heads)
            q_scale = (QK_NOPE + QK_ROPE) ** -0.5 * math.log2(math.e)
            cos, sin = _rope_tables(n)
            ones = _pad_to(jnp.ones((n_ctx, QK_ROPE), F32), LANES, 1)
            zeros = jnp.zeros((n_ctx, LANES), F32)
            kw = dict(heads=heads, q_lora=q_lora, kv_lora=kv_lora, q_scale=q_scale)
            kc, vct = _mla_proj(yt.reshape(bsz, n_ctx, d), m_y, norm_g[i, 1], ones, zeros,
                                (None,) * 4 + wts[4:], with_q=False, **kw)
            qt, kx, vxt = _mla_proj(xt.reshape(bsz, n, d), m_x, norm_g[i, 1], cos, sin, wts,
                                    with_q=True, **kw)
            att_t = _flash(qt, kc, vct, kx, vxt, heads=heads)
            xt = _attn_out(xt.reshape(bsz, n, d), att_t, mla_w_o[j].astype(BF16), m_x)
            xt = xt.reshape(bsz * n, d)
            assert last, "context attention output is only needed for non-final MLA layers"

        xt = _ffn(xt, m_x, 2, norm_g[i, 2], *ffn_w, i, 1, final_g=final_g if last else None)
        if not last:
            yt = _ffn(yt, m_y, 2, norm_g[i, 2], *ffn_w, i, 1)

    return xt.reshape(bsz, n, d)
```

```python
import functools
import math

import jax
import jax.numpy as jnp
from jax import lax
from jax.experimental import pallas as pl
from jax.experimental.pallas import tpu as pltpu

EPS = 1e-6
N_MOD = 9
GRID_W = 64
CHUNK = 128
GM_GROUPS = 8
QK_NOPE = 128
QK_ROPE = 64
V_DIM = 128
ROPE_THETA = 10000.0
HEAD_W = 256
LANES = 128
SUM_ROWS = 16
VT_ROWS = V_DIM + SUM_ROWS
SUBLANES = 8
MAX_CHAINS = 4
RING = 4
LOOKAHEAD = 2
FLASH_UNROLL = 16
ROW_CHUNK = 16
ROW_UNROLL = 8
VMEM_LIMIT = 56 * 1024 * 1024
VMEM_LIMIT_FFN = 60 * 1024 * 1024

BF16 = jnp.bfloat16
F32 = jnp.float32


def _cparams(sem, vmem=VMEM_LIMIT, flags=None):
    return pltpu.CompilerParams(dimension_semantics=sem, vmem_limit_bytes=vmem, flags=flags)


def _rms(x, width=None):
    n = x.shape[-1] if width is None else width
    return x * lax.rsqrt(jnp.sum(x * x, axis=-1, keepdims=True) * (1.0 / n) + EPS)


def _premod(x, mod_ref, k, g):
    y = _rms(x) * g
    return y * (1.0 + mod_ref[0, 3 * k + 1:3 * k + 2, :]) + mod_ref[0, 3 * k:3 * k + 1, :]


def _adaln_kernel(c_ref, w_ref, b_ref, o_ref):
    c = c_ref[...]
    s = c * jax.nn.sigmoid(c)
    o_ref[0] = jnp.dot(s, w_ref[0], preferred_element_type=F32,
                       precision=lax.Precision.HIGHEST) + b_ref[0]


def _adaln(cond8, mod_w, mod_b):
    depth, d, n = mod_w.shape
    tn = min(d, 1024)
    assert n % tn == 0
    return pl.pallas_call(
        _adaln_kernel,
        grid=(depth, n // tn),
        in_specs=[
            pl.BlockSpec((8, d), lambda l, j: (0, 0)),
            pl.BlockSpec((1, d, tn), lambda l, j: (l, 0, j)),
            pl.BlockSpec((1, 1, tn), lambda l, j: (l, 0, j)),
        ],
        out_specs=pl.BlockSpec((1, 8, tn), lambda l, j: (l, 0, j)),
        out_shape=jax.ShapeDtypeStruct((depth, 8, n), F32),
        compiler_params=_cparams(("parallel", "parallel")),
        name="adaln",
    )(cond8, mod_w, mod_b.reshape(depth, 1, n))


def _ffn_kernel(*refs, k, final):
    if final:
        x_ref, mod_ref, g_ref, wg_ref, wu_ref, wd_ref, fg_ref, o_ref, h_ref, gs_ref, sh_ref = refs
    else:
        x_ref, mod_ref, g_ref, wg_ref, wu_ref, wd_ref, o_ref, h_ref, gs_ref, sh_ref = refs
    f = pl.program_id(1)

    n_row_chunks = o_ref.shape[0] // ROW_CHUNK

    def row_chunk(r):
        return pl.ds(pl.multiple_of(r * ROW_CHUNK, ROW_CHUNK), ROW_CHUNK)

    @pl.when(f == 0)
    def _():
        gain = g_ref[...] * (1.0 + mod_ref[0, 3 * k + 1:3 * k + 2, :])
        gs_ref[...] = jnp.broadcast_to(gain, gs_ref.shape)
        sh_ref[...] = jnp.broadcast_to(mod_ref[0, 3 * k:3 * k + 1, :], sh_ref.shape)

        def chunk(r, carry):
            x = x_ref[row_chunk(r), :]
            h_ref[row_chunk(r), :] = (_rms(x) * gs_ref[...] + sh_ref[...]).astype(BF16)
            o_ref[row_chunk(r), :] = x
            return carry
        lax.fori_loop(0, n_row_chunks, chunk, 0, unroll=ROW_UNROLL)

    h = h_ref[...]
    tf = wg_ref.shape[1]
    halves = [slice(0, tf // 2), slice(tf // 2, tf)] if tf % (2 * LANES) == 0 else [slice(0, tf)]
    ab = [(jnp.dot(h, wg_ref[:, c], preferred_element_type=F32),
           jnp.dot(h, wu_ref[:, c], preferred_element_type=F32)) for c in halves]
    p = jnp.concatenate([(a * jax.nn.sigmoid(a) * b).astype(BF16) for a, b in ab], axis=1)
    half_gate = 0.5 * mod_ref[0, 3 * k + 2:3 * k + 3, :]
    o_ref[...] += jnp.dot(p, wd_ref[...], preferred_element_type=F32) * half_gate

    if final:
        @pl.when(f == pl.num_programs(1) - 1)
        def _():
            o_ref[...] = _rms(o_ref[...]) * fg_ref[...]


def _ffn(x, mod, k, g, wg, wu, wd, layer, which, final_g=None):
    t, d = x.shape
    nb = mod.shape[0]
    ff = wg.shape[-1]
    tm = min(1024, t // nb)
    tf = min(512, ff)
    tiles_per_mod = (t // nb) // tm
    final = final_g is not None
    in_specs = [
        pl.BlockSpec((tm, d), lambda i, f: (i, 0)),
        pl.BlockSpec((1, N_MOD, d), lambda i, f: (i // tiles_per_mod, 0, 0)),
        pl.BlockSpec((1, d), lambda i, f: (0, 0)),
        pl.BlockSpec((None, None, d, tf), lambda i, f: (layer, which, 0, f)),
        pl.BlockSpec((None, None, d, tf), lambda i, f: (layer, which, 0, f)),
        pl.BlockSpec((None, None, tf, d), lambda i, f: (layer, which, f, 0)),
    ]
    args = [x, mod, g.reshape(1, d), wg, wu, wd]
    if final:
        in_specs.append(pl.BlockSpec((1, d), lambda i, f: (0, 0)))
        args.append(final_g.reshape(1, d))
    return pl.pallas_call(
        functools.partial(_ffn_kernel, k=k, final=final),
        grid=(t // tm, ff // tf),
        in_specs=in_specs,
        out_specs=pl.BlockSpec((tm, d), lambda i, f: (i, 0)),
        out_shape=jax.ShapeDtypeStruct((t, d), F32),
        scratch_shapes=[pltpu.VMEM((tm, d), BF16), pltpu.VMEM((ROW_CHUNK, d), F32),
                        pltpu.VMEM((ROW_CHUNK, d), F32)],
        compiler_params=_cparams(("parallel", "arbitrary"), VMEM_LIMIT_FFN),
        name="ffn",
    )(*args)


def _proj_res_kernel(x_ref, a_ref, w_ref, mod_ref, o_ref):
    y = jnp.dot(a_ref[...], w_ref[...], preferred_element_type=F32)
    o_ref[...] = x_ref[...] + mod_ref[0, 5:6, :] * y


def _proj_res(x, a, w, mod):
    t, d = x.shape
    nb = mod.shape[0]
    din = a.shape[1]
    tm = min(512, t // nb)
    tiles_per_mod = (t // nb) // tm
    return pl.pallas_call(
        _proj_res_kernel,
        grid=(t // tm,),
        in_specs=[
            pl.BlockSpec((tm, d), lambda i: (i, 0)),
            pl.BlockSpec((tm, din), lambda i: (i, 0)),
            pl.BlockSpec((din, d), lambda i: (0, 0)),
            pl.BlockSpec((1, N_MOD, d), lambda i: (i // tiles_per_mod, 0, 0)),
        ],
        out_specs=pl.BlockSpec((tm, d), lambda i: (i, 0)),
        out_shape=jax.ShapeDtypeStruct((t, d), F32),
        compiler_params=_cparams(("parallel",)),
        name="proj_res",
    )(x, a, w, mod)


def _gmlp_kernel(x_ref, mod_ref, g_ref, wu_ref, wv_ref, lng_ref, lnb_ref, ws_ref,
                 bs_ref, o_ref, v_ref, u_ref):
    tm, inner = o_ref.shape
    gdim = inner // GM_GROUPS
    h = _premod(x_ref[...], mod_ref, 1, g_ref[...]).astype(BF16)
    zv = jnp.dot(h, wv_ref[...], preferred_element_type=F32)
    zu = jnp.dot(h, wu_ref[...], preferred_element_type=F32)
    v = jax.nn.gelu(zv)
    mu = jnp.mean(v, axis=-1, keepdims=True)
    vc = v - mu
    var = jnp.mean(vc * vc, axis=-1, keepdims=True)
    v_ref[...] = ((vc * lax.rsqrt(var + EPS)) * lng_ref[...] + lnb_ref[...]).astype(BF16)
    u_ref[...] = jax.nn.gelu(zu)
    for c in range(tm // CHUNK):
        rows = slice(c * CHUNK, (c + 1) * CHUNK)
        for gi in range(GM_GROUPS):
            cols = slice(gi * gdim, (gi + 1) * gdim)
            mixed = jnp.dot(ws_ref[gi], v_ref[rows, cols], preferred_element_type=F32)
            o_ref[rows, cols] = (u_ref[rows, cols] * (mixed + bs_ref[:, cols])).astype(BF16)


def _gmlp_gate(x, mod, g, w_u, w_v, ln_g, ln_b, w_s, b_s_full):
    t, d = x.shape
    nb = mod.shape[0]
    inner = w_u.shape[1]
    tm = min(512, t // nb)
    tiles_per_mod = (t // nb) // tm
    const2 = lambda i: (0, 0)
    return pl.pallas_call(
        _gmlp_kernel,
        grid=(t // tm,),
        in_specs=[
            pl.BlockSpec((tm, d), lambda i: (i, 0)),
            pl.BlockSpec((1, N_MOD, d), lambda i: (i // tiles_per_mod, 0, 0)),
            pl.BlockSpec((1, d), const2),
            pl.BlockSpec((d, inner), const2),
            pl.BlockSpec((d, inner), const2),
            pl.BlockSpec((1, inner), const2),
            pl.BlockSpec((1, inner), const2),
            pl.BlockSpec((GM_GROUPS, CHUNK, CHUNK), lambda i: (0, 0, 0)),
            pl.BlockSpec((CHUNK, inner), const2),
        ],
        out_specs=pl.BlockSpec((tm, inner), lambda i: (i, 0)),
        out_shape=jax.ShapeDtypeStruct((t, inner), BF16),
        scratch_shapes=[pltpu.VMEM((tm, inner), BF16), pltpu.VMEM((tm, inner), F32)],
        compiler_params=_cparams(("parallel",)),
        name="gmlp_gate",
    )(x, mod, g.reshape(1, d), w_u, w_v, ln_g.reshape(1, inner), ln_b.reshape(1, inner),
      w_s, b_s_full)


def _mla_proj_kernel(*refs, heads, q_lora, kv_lora, q_scale, with_q):
    if with_q:
        (x_ref, mod_ref, g_ref, cos_ref, sin_ref, cost_ref, sint_ref, wdq_ref, qn_ref, wqt_ref,
         wqrt_ref, wkv_ref, kvn_ref, wuk_ref, wuvt_ref, qt_ref, k_ref, vt_ref) = refs
    else:
        (x_ref, mod_ref, g_ref, cos_ref, sin_ref,
         wkv_ref, kvn_ref, wuk_ref, wuvt_ref, k_ref, vt_ref) = refs
    h = _premod(x_ref[0], mod_ref, 1, g_ref[...]).astype(BF16)

    kv = jnp.dot(h, wkv_ref[...], preferred_element_type=F32)
    if with_q:
        ql = jnp.dot(h, wdq_ref[...], preferred_element_type=F32)
    c_kv = _rms(kv[:, :2 * LANES], kv_lora) * kvn_ref[...]
    k_pe = (kv[:, 2 * LANES:3 * LANES] * cos_ref[...] + kv[:, 3 * LANES:] * sin_ref[...]).astype(BF16)

    up_k = jnp.dot(c_kv.astype(BF16), wuk_ref[...], preferred_element_type=F32)
    v_t = jnp.dot(wuvt_ref[...], c_kv.T.astype(BF16), preferred_element_type=F32).astype(BF16)
    if with_q:
        qn_t = (_rms(ql, q_lora) * qn_ref[...] * q_scale).T.astype(BF16)
        qa_t = jnp.dot(wqt_ref[...], qn_t, preferred_element_type=F32)
        qr_t = jnp.dot(wqrt_ref[...], qn_t, preferred_element_type=F32)
    for hd in range(heads):
        k_ref[0, :, hd * HEAD_W:hd * HEAD_W + LANES] = k_pe
        k_ref[0, :, hd * HEAD_W + LANES:(hd + 1) * HEAD_W] = (
            up_k[:, hd * QK_NOPE:(hd + 1) * QK_NOPE].astype(BF16))
    ones = jnp.ones((SUM_ROWS, v_t.shape[1]), BF16)
    for hd in range(heads):
        vt_ref[0, hd * VT_ROWS:hd * VT_ROWS + V_DIM, :] = v_t[hd * V_DIM:(hd + 1) * V_DIM]
        vt_ref[0, hd * VT_ROWS + V_DIM:(hd + 1) * VT_ROWS, :] = ones

    if with_q:
        cos_t = cost_ref[...]
        sin_t = sint_ref[...]
        for hd in range(heads):
            pe = (qa_t[hd * HEAD_W:hd * HEAD_W + LANES] * cos_t
                  + qr_t[hd * LANES:(hd + 1) * LANES] * sin_t)
            qt_ref[0, hd * HEAD_W:hd * HEAD_W + LANES, :] = pe.astype(BF16)
            qt_ref[0, hd * HEAD_W + LANES:(hd + 1) * HEAD_W, :] = (
                qa_t[hd * HEAD_W + LANES:(hd + 1) * HEAD_W].astype(BF16))


def _mla_proj(x3, mod, g, cos, sin, wts, *, heads, q_lora, kv_lora, q_scale, with_q):
    bsz, n, d = x3.shape
    nb = mod.shape[0]
    tm = min(256, n)
    mod_idx = (lambda b, i: (b, 0, 0)) if nb == bsz else (lambda b, i: (0, 0, 0))
    const2 = lambda b, i: (0, 0)
    w_dq, q_norm, w_qt, w_qrt, w_kv, kv_norm, w_uk, w_uvt = wts
    in_specs = [
        pl.BlockSpec((1, tm, d), lambda b, i: (b, i, 0)),
        pl.BlockSpec((1, N_MOD, d), mod_idx),
        pl.BlockSpec((1, d), const2),
        pl.BlockSpec((tm, LANES), lambda b, i: (i, 0)),
        pl.BlockSpec((tm, LANES), lambda b, i: (i, 0)),
    ]
    args = [x3, mod, g.reshape(1, d), cos, sin]
    if with_q:
        in_specs += [pl.BlockSpec((LANES, tm), lambda b, i: (0, i)),
                     pl.BlockSpec((LANES, tm), lambda b, i: (0, i))]
        args += [cos.T, sin.T]
        in_specs += [pl.BlockSpec(w.shape, const2) for w in (w_dq, q_norm, w_qt, w_qrt)]
        args += [w_dq, q_norm, w_qt, w_qrt]
    in_specs += [pl.BlockSpec(w.shape, const2) for w in (w_kv, kv_norm, w_uk, w_uvt)]
    args += [w_kv, kv_norm, w_uk, w_uvt]
    out_specs = [pl.BlockSpec((1, tm, heads * HEAD_W), lambda b, i: (b, i, 0)),
                 pl.BlockSpec((1, heads * VT_ROWS, tm), lambda b, i: (b, 0, i))]
    out_shape = [jax.ShapeDtypeStruct((bsz, n, heads * HEAD_W), BF16),
                 jax.ShapeDtypeStruct((bsz, heads * VT_ROWS, n), BF16)]
    if with_q:
        out_specs = [pl.BlockSpec((1, heads * HEAD_W, tm), lambda b, i: (b, 0, i))] + out_specs
        out_shape = [jax.ShapeDtypeStruct((bsz, heads * HEAD_W, n), BF16)] + out_shape
    return pl.pallas_call(
        functools.partial(_mla_proj_kernel, heads=heads, q_lora=q_lora, kv_lora=kv_lora,
                          q_scale=q_scale, with_q=with_q),
        grid=(bsz, n // tm),
        in_specs=in_specs,
        out_specs=out_specs,
        out_shape=out_shape,
        compiler_params=_cparams(("parallel", "parallel")),
        name="mla_proj_q" if with_q else "mla_proj_kv",
    )(*args)


def _col_max(s):
    rows = s.shape[0]
    parts = [s[r * SUBLANES:(r + 1) * SUBLANES] for r in range(MAX_CHAINS)]
    for r in range(MAX_CHAINS, rows // SUBLANES):
        parts[r % MAX_CHAINS] = jnp.maximum(parts[r % MAX_CHAINS], s[r * SUBLANES:(r + 1) * SUBLANES])
    while len(parts) > 1:
        parts = [jnp.maximum(parts[i], parts[i + 1]) for i in range(0, len(parts), 2)]
    return jnp.max(parts[0], axis=0, keepdims=True)


def _flash_kernel(qt_ref, kc_ref, vct_ref, kx_ref, vxt_ref, o_ref, *scratch, tk, unroll):
    s_bufs, acc_ref = scratch[:RING], scratch[RING]
    qt = qt_ref[0]
    n_chunks = kx_ref.shape[1] // tk

    def pv(vt, p):
        return jnp.dot(vt, p, preferred_element_type=F32)

    def probs(s, m):
        return jnp.exp2((s - m).astype(BF16))

    s_ctx = jnp.dot(kc_ref[0], qt, preferred_element_type=F32)
    cmaxes = []
    for c in range(LOOKAHEAD):
        s = jnp.dot(kx_ref[0, c * tk:(c + 1) * tk, :], qt, preferred_element_type=F32)
        s_bufs[c][...] = s
        cmaxes.append(_col_max(s))
    m = _col_max(s_ctx)
    acc_ref[...] = pv(vct_ref[0], probs(s_ctx, m))

    def step(j, u, carry):
        m, cmax, *cmax_ahead = carry
        cur = pl.multiple_of(j * tk, tk)
        nxt = pl.multiple_of(jnp.minimum(j + LOOKAHEAD, n_chunks - 1) * tk, tk)
        m_new = jnp.maximum(m, cmax)
        p = probs(s_bufs[u][...], m_new)
        acc_ref[...] = jnp.exp2(m - m_new) * acc_ref[...] + pv(vxt_ref[0, :, pl.ds(cur, tk)], p)
        s_next = jnp.dot(kx_ref[0, pl.ds(nxt, tk), :], qt, preferred_element_type=F32)
        s_bufs[(u + LOOKAHEAD) % RING][...] = s_next
        return (m_new, *cmax_ahead, _col_max(s_next))

    def body(i, carry):
        for u in range(unroll):
            carry = step(unroll * i + u, u % RING, carry)
        return carry

    lax.fori_loop(0, n_chunks // unroll, body, (m, *cmaxes))
    acc = acc_ref[...]
    o_ref[0] = (acc[:V_DIM] / acc[V_DIM:V_DIM + 1]).astype(BF16)


def _flash(qt, kc, vct, kx, vxt, *, heads):
    bsz, _, n = qt.shape
    nc = kc.shape[1]
    tq = min(512, n)
    tk = min(512, n // RING)
    unroll = FLASH_UNROLL if (n // tk) % FLASH_UNROLL == 0 else RING
    assert (n // tk) % unroll == 0
    return pl.pallas_call(
        functools.partial(_flash_kernel, tk=tk, unroll=unroll),
        grid=(bsz, heads, n // tq),
        in_specs=[
            pl.BlockSpec((1, HEAD_W, tq), lambda b, h, i: (b, h, i)),
            pl.BlockSpec((1, nc, HEAD_W), lambda b, h, i: (b, 0, h)),
            pl.BlockSpec((1, VT_ROWS, nc), lambda b, h, i: (b, h, 0)),
            pl.BlockSpec((1, n, HEAD_W), lambda b, h, i: (b, 0, h)),
            pl.BlockSpec((1, VT_ROWS, n), lambda b, h, i: (b, h, 0)),
        ],
        out_specs=pl.BlockSpec((1, V_DIM, tq), lambda b, h, i: (b, h, i)),
        out_shape=jax.ShapeDtypeStruct((bsz, heads * V_DIM, n), BF16),
        scratch_shapes=[pltpu.VMEM((tk, tq), F32)] * RING + [pltpu.VMEM((VT_ROWS, tq), F32)],
        compiler_params=_cparams(("parallel", "parallel", "arbitrary")),
        name="flash",
    )(qt, kc, vct, kx, vxt)


def _attn_out_kernel(x_ref, at_ref, w_ref, mod_ref, o_ref):
    y = lax.dot_general(at_ref[0], w_ref[...], (((0,), (0,)), ((), ())),
                        preferred_element_type=F32)
    o_ref[0] = x_ref[0] + mod_ref[0, 5:6, :] * y


def _attn_out(x3, at, w, mod):
    bsz, n, d = x3.shape
    din = at.shape[1]
    tm = min(512, n)
    return pl.pallas_call(
        _attn_out_kernel,
        grid=(bsz, n // tm),
        in_specs=[
            pl.BlockSpec((1, tm, d), lambda b, i: (b, i, 0)),
            pl.BlockSpec((1, din, tm), lambda b, i: (b, 0, i)),
            pl.BlockSpec((din, d), lambda b, i: (0, 0)),
            pl.BlockSpec((1, N_MOD, d), lambda b, i: (b, 0, 0)),
        ],
        out_specs=pl.BlockSpec((1, tm, d), lambda b, i: (b, i, 0)),
        out_shape=jax.ShapeDtypeStruct((bsz, n, d), F32),
        compiler_params=_cparams(("parallel", "parallel")),
        name="attn_out",
    )(x3, at, w, mod)


def _rope_rot_cols(w):
    q = QK_ROPE // 4
    parts = w.reshape(*w.shape[:-1], 2, 2, q)
    rot = jnp.stack([-parts[..., 1, :], parts[..., 0, :]], axis=-2)
    return rot.reshape(w.shape)


def _pad_to(a, size, axis):
    pad = [(0, 0)] * a.ndim
    pad[axis] = (0, size - a.shape[axis])
    return jnp.pad(a, pad)


def _mla_weights(w_dq, q_norm, w_uq, w_dkv, kv_norm, w_ukv, heads):
    d, q_lora = w_dq.shape
    kv_lora = kv_norm.shape[0]
    qk_dim = QK_NOPE + QK_ROPE
    ql_pad = -(-q_lora // LANES) * LANES
    w_dq_p = _pad_to(w_dq, ql_pad, 1).astype(BF16)
    q_norm_p = _pad_to(q_norm, ql_pad, 0).reshape(1, ql_pad)
    uq = w_uq.reshape(q_lora, heads, qk_dim)
    uq_nope, uq_pe = uq[..., :QK_NOPE], uq[..., QK_NOPE:]
    zeros = jnp.zeros((q_lora, heads, LANES - QK_ROPE), w_uq.dtype)
    w_q = jnp.concatenate([uq_pe, zeros, uq_nope], axis=-1).reshape(q_lora, heads * HEAD_W)
    w_qr = jnp.concatenate([_rope_rot_cols(uq_pe), zeros], axis=-1).reshape(q_lora, heads * LANES)
    w_q = _pad_to(w_q, ql_pad, 0).astype(BF16)
    w_qr = _pad_to(w_qr, ql_pad, 0).astype(BF16)
    w_c = _pad_to(w_dkv[:, :kv_lora], 2 * LANES, 1)
    w_pe = w_dkv[:, kv_lora:]
    w_kv = jnp.concatenate([w_c, _pad_to(w_pe, LANES, 1), _pad_to(_rope_rot_cols(w_pe), LANES, 1)],
                           axis=1).astype(BF16)
    kv_norm_p = _pad_to(kv_norm, 2 * LANES, 0).reshape(1, 2 * LANES)
    ukv = w_ukv.reshape(kv_lora, heads, QK_NOPE + V_DIM)
    w_ukv_p = jnp.concatenate([ukv[..., :QK_NOPE].reshape(kv_lora, heads * QK_NOPE),
                               ukv[..., QK_NOPE:].reshape(kv_lora, heads * V_DIM)], axis=1)
    w_ukv_p = _pad_to(w_ukv_p, 2 * LANES, 0).astype(BF16)
    nope_w = heads * QK_NOPE
    w_uk, w_uvt = w_ukv_p[:, :nope_w], w_ukv_p[:, nope_w:].T
    return (w_dq_p, q_norm_p, w_q.T, w_qr.T, w_kv, kv_norm_p, w_uk, w_uvt), q_lora, kv_lora


def _rope_tables(n):
    t = jnp.arange(n)
    row = (t // GRID_W).astype(F32)
    col = (t % GRID_W).astype(F32)
    half = QK_ROPE // 4
    inv = ROPE_THETA ** (-jnp.arange(half, dtype=F32) / half)
    ang_r = row[:, None] * inv
    ang_c = col[:, None] * inv
    ang = jnp.concatenate([ang_r, ang_r, ang_c, ang_c], axis=-1)
    cos = _pad_to(jnp.cos(ang), LANES, 1)
    sin = _pad_to(jnp.sin(ang), LANES, 1)
    return cos, sin


def kernel(x, c, ctx, c_ctx, mod_w, mod_b, norm_g, ffn_w_gate, ffn_w_up, ffn_w_down,
           gm_w_in, gm_ln_g, gm_ln_b, gm_w_s, gm_b_s, gm_w_out,
           mla_w_dq, mla_q_norm, mla_w_uq, mla_w_dkv, mla_kv_norm, mla_w_ukv, mla_w_o,
           final_g):
    bsz, n, d = x.shape
    n_ctx = ctx.shape[1]
    depth = mod_w.shape[0]
    heads = d // V_DIM
    assert bsz + 1 <= 8

    cond8 = jnp.concatenate([c, c_ctx[None], jnp.zeros((8 - bsz - 1, d), F32)], axis=0)
    mods = _adaln(cond8, mod_w, mod_b).reshape(depth, 8, N_MOD, d)

    xt = x.reshape(bsz * n, d)
    yt = ctx.reshape(bsz * n_ctx, d)
    ffn_w = (ffn_w_gate.astype(BF16), ffn_w_up.astype(BF16), ffn_w_down.astype(BF16))

    for i in range(depth):
        last = i == depth - 1
        use_a = i % 2 == 0
        j = i // 2
        m_x = mods[i, :bsz]
        m_y = mods[i, bsz:bsz + 1]
        ctx_needed = (not last) or (not use_a)

        xt = _ffn(xt, m_x, 0, norm_g[i, 0], *ffn_w, i, 0)
        if ctx_needed:
            yt = _ffn(yt, m_y, 0, norm_g[i, 0], *ffn_w, i, 0)

        if use_a:
            inner = gm_w_out.shape[1]
            w_u = gm_w_in[j, :, :inner].astype(BF16)
            w_v = gm_w_in[j, :, inner:].astype(BF16)
            w_s = gm_w_s[j].astype(BF16)
            b_full = jnp.repeat(gm_b_s[j].T, inner // GM_GROUPS, axis=1)
            w_out = gm_w_out[j].astype(BF16)
            gm = (norm_g[i, 1], w_u, w_v, gm_ln_g[j], gm_ln_b[j], w_s, b_full)
            xt = _proj_res(xt, _gmlp_gate(xt, m_x, *gm), w_out, m_x)
            if not last:
                yt = _proj_res(yt, _gmlp_gate(yt, m_y, *gm), w_out, m_y)
        else:
            wts, q_lora, kv_lora = _mla_weights(mla_w_dq[j], mla_q_norm[j], mla_w_uq[j],
                                                mla_w_dkv[j], mla_kv_norm[j], mla_w_ukv[j], heads)
            q_scale = (QK_NOPE + QK_ROPE) ** -0.5 * math.log2(math.e)
            cos, sin = _rope_tables(n)
            ones = _pad_to(jnp.ones((n_ctx, QK_ROPE), F32), LANES, 1)
            zeros = jnp.zeros((n_ctx, LANES), F32)
            kw = dict(heads=heads, q_lora=q_lora, kv_lora=kv_lora, q_scale=q_scale)
            kc, vct = _mla_proj(yt.reshape(bsz, n_ctx, d), m_y, norm_g[i, 1], ones, zeros,
                                (None,) * 4 + wts[4:], with_q=False, **kw)
            qt, kx, vxt = _mla_proj(xt.reshape(bsz, n, d), m_x, norm_g[i, 1], cos, sin, wts,
                                    with_q=True, **kw)
            att_t = _flash(qt, kc, vct, kx, vxt, heads=heads)
            xt = _attn_out(xt.reshape(bsz, n, d), att_t, mla_w_o[j].astype(BF16), m_x)
            xt = xt.reshape(bsz * n, d)
            assert last, "context attention output is only needed for non-final MLA layers"

        xt = _ffn(xt, m_x, 2, norm_g[i, 2], *ffn_w, i, 1, final_g=final_g if last else None)
        if not last:
            yt = _ffn(yt, m_y, 2, norm_g[i, 2], *ffn_w, i, 1)

    return xt.reshape(bsz, n, d)
```

```python
import functools
import math

import jax
import jax.numpy as jnp
from jax import lax
from jax.experimental import pallas as pl
from jax.experimental.pallas import tpu as pltpu

EPS = 1e-6
N_MOD = 9
GRID_W = 64
CHUNK = 128
GM_GROUPS = 8
QK_NOPE = 128
QK_ROPE = 64
V_DIM = 128
ROPE_THETA = 10000.0
HEAD_W = 256
LANES = 128
SUM_ROWS = 16
VT_ROWS = V_DIM + SUM_ROWS
SUBLANES = 8
MAX_CHAINS = 4
RING = 4
LOOKAHEAD = 2
FLASH_UNROLL = 16
ROW_CHUNK = 16
ROW_UNROLL = 8
VMEM_LIMIT = 56 * 1024 * 1024
VMEM_LIMIT_FFN = 60 * 1024 * 1024

BF16 = jnp.bfloat16
F32 = jnp.float32


def _cparams(sem, vmem=VMEM_LIMIT, flags=None):
    return pltpu.CompilerParams(dimension_semantics=sem, vmem_limit_bytes=vmem, flags=flags)


def _rms(x, width=None):
    n = x.shape[-1] if width is None else width
    return x * lax.rsqrt(jnp.sum(x * x, axis=-1, keepdims=True) * (1.0 / n) + EPS)


def _premod(x, mod_ref, k, g):
    y = _rms(x) * g
    return y * (1.0 + mod_ref[0, 3 * k + 1:3 * k + 2, :]) + mod_ref[0, 3 * k:3 * k + 1, :]


def _adaln_kernel(c_ref, w_ref, b_ref, o_ref):
    c = c_ref[...]
    s = c * jax.nn.sigmoid(c)
    o_ref[0] = jnp.dot(s, w_ref[0], preferred_element_type=F32,
                       precision=lax.Precision.HIGHEST) + b_ref[0]


def _adaln(cond8, mod_w, mod_b):
    depth, d, n = mod_w.shape
    tn = min(d, 1024)
    assert n % tn == 0
    return pl.pallas_call(
        _adaln_kernel,
        grid=(depth, n // tn),
        in_specs=[
            pl.BlockSpec((8, d), lambda l, j: (0, 0)),
            pl.BlockSpec((1, d, tn), lambda l, j: (l, 0, j)),
            pl.BlockSpec((1, 1, tn), lambda l, j: (l, 0, j)),
        ],
        out_specs=pl.BlockSpec((1, 8, tn), lambda l, j: (l, 0, j)),
        out_shape=jax.ShapeDtypeStruct((depth, 8, n), F32),
        compiler_params=_cparams(("parallel", "parallel")),
        name="adaln",
    )(cond8, mod_w, mod_b.reshape(depth, 1, n))


def _ffn_kernel(*refs, k, final):
    if final:
        x_ref, mod_ref, g_ref, wg_ref, wu_ref, wd_ref, fg_ref, o_ref, h_ref, gs_ref, sh_ref = refs
    else:
        x_ref, mod_ref, g_ref, wg_ref, wu_ref, wd_ref, o_ref, h_ref, gs_ref, sh_ref = refs
    f = pl.program_id(1)

    n_row_chunks = o_ref.shape[0] // ROW_CHUNK

    def row_chunk(r):
        return pl.ds(pl.multiple_of(r * ROW_CHUNK, ROW_CHUNK), ROW_CHUNK)

    @pl.when(f == 0)
    def _():
        gain = g_ref[...] * (1.0 + mod_ref[0, 3 * k + 1:3 * k + 2, :])
        gs_ref[...] = jnp.broadcast_to(gain, gs_ref.shape)
        sh_ref[...] = jnp.broadcast_to(mod_ref[0, 3 * k:3 * k + 1, :], sh_ref.shape)

        def chunk(r, carry):
            x = x_ref[row_chunk(r), :]
            h_ref[row_chunk(r), :] = (_rms(x) * gs_ref[...] + sh_ref[...]).astype(BF16)
            o_ref[row_chunk(r), :] = x
            return carry
        lax.fori_loop(0, n_row_chunks, chunk, 0, unroll=ROW_UNROLL)

    h = h_ref[...]
    tf = wg_ref.shape[1]
    halves = [slice(0, tf // 2), slice(tf // 2, tf)] if tf % (2 * LANES) == 0 else [slice(0, tf)]
    ab = [(jnp.dot(h, wg_ref[:, c], preferred_element_type=F32),
           jnp.dot(h, wu_ref[:, c], preferred_element_type=F32)) for c in halves]
    p = jnp.concatenate([(a * jax.nn.sigmoid(a) * b).astype(BF16) for a, b in ab], axis=1)
    half_gate = 0.5 * mod_ref[0, 3 * k + 2:3 * k + 3, :]
    o_ref[...] += jnp.dot(p, wd_ref[...], preferred_element_type=F32) * half_gate

    if final:
        @pl.when(f == pl.num_programs(1) - 1)
        def _():
            o_ref[...] = _rms(o_ref[...]) * fg_ref[...]


def _ffn(x, mod, k, g, wg, wu, wd, layer, which, final_g=None):
    t, d = x.shape
    nb = mod.shape[0]
    ff = wg.shape[-1]
    tm = min(1024, t // nb)
    tf = min(512, ff)
    tiles_per_mod = (t // nb) // tm
    final = final_g is not None
    in_specs = [
        pl.BlockSpec((tm, d), lambda i, f: (i, 0)),
        pl.BlockSpec((1, N_MOD, d), lambda i, f: (i // tiles_per_mod, 0, 0)),
        pl.BlockSpec((1, d), lambda i, f: (0, 0)),
        pl.BlockSpec((None, None, d, tf), lambda i, f: (layer, which, 0, f)),
        pl.BlockSpec((None, None, d, tf), lambda i, f: (layer, which, 0, f)),
        pl.BlockSpec((None, None, tf, d), lambda i, f: (layer, which, f, 0)),
    ]
    args = [x, mod, g.reshape(1, d), wg, wu, wd]
    if final:
        in_specs.append(pl.BlockSpec((1, d), lambda i, f: (0, 0)))
        args.append(final_g.reshape(1, d))
    return pl.pallas_call(
        functools.partial(_ffn_kernel, k=k, final=final),
        grid=(t // tm, ff // tf),
        in_specs=in_specs,
        out_specs=pl.BlockSpec((tm, d), lambda i, f: (i, 0)),
        out_shape=jax.ShapeDtypeStruct((t, d), F32),
        scratch_shapes=[pltpu.VMEM((tm, d), BF16), pltpu.VMEM((ROW_CHUNK, d), F32),
                        pltpu.VMEM((ROW_CHUNK, d), F32)],
        compiler_params=_cparams(("parallel", "arbitrary"), VMEM_LIMIT_FFN),
        name="ffn",
    )(*args)


def _proj_res_kernel(x_ref, a_ref, w_ref, mod_ref, o_ref):
    y = jnp.dot(a_ref[...], w_ref[...], preferred_element_type=F32)
    o_ref[...] = x_ref[...] + mod_ref[0, 5:6, :] * y


def _proj_res(x, a, w, mod):
    t, d = x.shape
    nb = mod.shape[0]
    din = a.shape[1]
    tm = min(512, t // nb)
    tiles_per_mod = (t // nb) // tm
    return pl.pallas_call(
        _proj_res_kernel,
        grid=(t // tm,),
        in_specs=[
            pl.BlockSpec((tm, d), lambda i: (i, 0)),
            pl.BlockSpec((tm, din), lambda i: (i, 0)),
            pl.BlockSpec((din, d), lambda i: (0, 0)),
            pl.BlockSpec((1, N_MOD, d), lambda i: (i // tiles_per_mod, 0, 0)),
        ],
        out_specs=pl.BlockSpec((tm, d), lambda i: (i, 0)),
        out_shape=jax.ShapeDtypeStruct((t, d), F32),
        compiler_params=_cparams(("parallel",)),
        name="proj_res",
    )(x, a, w, mod)


def _gmlp_kernel(x_ref, mod_ref, g_ref, wu_ref, wv_ref, lng_ref, lnb_ref, ws_ref,
                 bs_ref, o_ref, v_ref, u_ref):
    tm, inner = o_ref.shape
    gdim = inner // GM_GROUPS
    h = _premod(x_ref[...], mod_ref, 1, g_ref[...]).astype(BF16)
    zv = jnp.dot(h, wv_ref[...], preferred_element_type=F32)
    zu = jnp.dot(h, wu_ref[...], preferred_element_type=F32)
    v = jax.nn.gelu(zv)
    mu = jnp.mean(v, axis=-1, keepdims=True)
    vc = v - mu
    var = jnp.mean(vc * vc, axis=-1, keepdims=True)
    v_ref[...] = ((vc * lax.rsqrt(var + EPS)) * lng_ref[...] + lnb_ref[...]).astype(BF16)
    u_ref[...] = jax.nn.gelu(zu)
    for c in range(tm // CHUNK):
        rows = slice(c * CHUNK, (c + 1) * CHUNK)
        for gi in range(GM_GROUPS):
            cols = slice(gi * gdim, (gi + 1) * gdim)
            mixed = jnp.dot(ws_ref[gi], v_ref[rows, cols], preferred_element_type=F32)
            o_ref[rows, cols] = (u_ref[rows, cols] * (mixed + bs_ref[:, cols])).astype(BF16)


def _gmlp_gate(x, mod, g, w_u, w_v, ln_g, ln_b, w_s, b_s_full):
    t, d = x.shape
    nb = mod.shape[0]
    inner = w_u.shape[1]
    tm = min(512, t // nb)
    tiles_per_mod = (t // nb) // tm
    const2 = lambda i: (0, 0)
    return pl.pallas_call(
        _gmlp_kernel,
        grid=(t // tm,),
        in_specs=[
            pl.BlockSpec((tm, d), lambda i: (i, 0)),
            pl.BlockSpec((1, N_MOD, d), lambda i: (i // tiles_per_mod, 0, 0)),
            pl.BlockSpec((1, d), const2),
            pl.BlockSpec((d, inner), const2),
            pl.BlockSpec((d, inner), const2),
            pl.BlockSpec((1, inner), const2),
            pl.BlockSpec((1, inner), const2),
            pl.BlockSpec((GM_GROUPS, CHUNK, CHUNK), lambda i: (0, 0, 0)),
            pl.BlockSpec((CHUNK, inner), const2),
        ],
        out_specs=pl.BlockSpec((tm, inner), lambda i: (i, 0)),
        out_shape=jax.ShapeDtypeStruct((t, inner), BF16),
        scratch_shapes=[pltpu.VMEM((tm, inner), BF16), pltpu.VMEM((tm, inner), F32)],
        compiler_params=_cparams(("parallel",)),
        name="gmlp_gate",
    )(x, mod, g.reshape(1, d), w_u, w_v, ln_g.reshape(1, inner), ln_b.reshape(1, inner),
      w_s, b_s_full)


def _mla_proj_kernel(*refs, heads, q_lora, kv_lora, q_scale, with_q):
    if with_q:
        (x_ref, mod_ref, g_ref, cos_ref, sin_ref, cost_ref, sint_ref, wdq_ref, qn_ref, wqt_ref,
         wqrt_ref, wkv_ref, kvn_ref, wuk_ref, wuvt_ref, qt_ref, k_ref, vt_ref) = refs
    else:
        (x_ref, mod_ref, g_ref, cos_ref, sin_ref,
         wkv_ref, kvn_ref, wuk_ref, wuvt_ref, k_ref, vt_ref) = refs
    h = _premod(x_ref[0], mod_ref, 1, g_ref[...]).astype(BF16)

    kv = jnp.dot(h, wkv_ref[...], preferred_element_type=F32)
    if with_q:
        ql = jnp.dot(h, wdq_ref[...], preferred_element_type=F32)
    c_kv = _rms(kv[:, :2 * LANES], kv_lora) * kvn_ref[...]
    k_pe = (kv[:, 2 * LANES:3 * LANES] * cos_ref[...] + kv[:, 3 * LANES:] * sin_ref[...]).astype(BF16)

    up_k = jnp.dot(c_kv.astype(BF16), wuk_ref[...], preferred_element_type=F32)
    v_t = jnp.dot(wuvt_ref[...], c_kv.T.astype(BF16), preferred_element_type=F32).astype(BF16)
    if with_q:
        qn_t = (_rms(ql, q_lora) * qn_ref[...] * q_scale).T.astype(BF16)
        qa_t = jnp.dot(wqt_ref[...], qn_t, preferred_element_type=F32)
        qr_t = jnp.dot(wqrt_ref[...], qn_t, preferred_element_type=F32)
    for hd in range(heads):
        k_ref[0, :, hd * HEAD_W:hd * HEAD_W + LANES] = k_pe
        k_ref[0, :, hd * HEAD_W + LANES:(hd + 1) * HEAD_W] = (
            up_k[:, hd * QK_NOPE:(hd + 1) * QK_NOPE].astype(BF16))
    ones = jnp.ones((SUM_ROWS, v_t.shape[1]), BF16)
    for hd in range(heads):
        vt_ref[0, hd * VT_ROWS:hd * VT_ROWS + V_DIM, :] = v_t[hd * V_DIM:(hd + 1) * V_DIM]
        vt_ref[0, hd * VT_ROWS + V_DIM:(hd + 1) * VT_ROWS, :] = ones

    if with_q:
        cos_t = cost_ref[...]
        sin_t = sint_ref[...]
        for hd in range(heads):
            pe = (qa_t[hd * HEAD_W:hd * HEAD_W + LANES] * cos_t
                  + qr_t[hd * LANES:(hd + 1) * LANES] * sin_t)
            qt_ref[0, hd * HEAD_W:hd * HEAD_W + LANES, :] = pe.astype(BF16)
            qt_ref[0, hd * HEAD_W + LANES:(hd + 1) * HEAD_W, :] = (
                qa_t[hd * HEAD_W + LANES:(hd + 1) * HEAD_W].astype(BF16))


def _mla_proj(x3, mod, g, cos, sin, wts, *, heads, q_lora, kv_lora, q_scale, with_q):
    bsz, n, d = x3.shape
    nb = mod.shape[0]
    tm = min(256, n)
    mod_idx = (lambda b, i: (b, 0, 0)) if nb == bsz else (lambda b, i: (0, 0, 0))
    const2 = lambda b, i: (0, 0)
    w_dq, q_norm, w_qt, w_qrt, w_kv, kv_norm, w_uk, w_uvt = wts
    in_specs = [
        pl.BlockSpec((1, tm, d), lambda b, i: (b, i, 0)),
        pl.BlockSpec((1, N_MOD, d), mod_idx),
        pl.BlockSpec((1, d), const2),
        pl.BlockSpec((tm, LANES), lambda b, i: (i, 0)),
        pl.BlockSpec((tm, LANES), lambda b, i: (i, 0)),
    ]
    args = [x3, mod, g.reshape(1, d), cos, sin]
    if with_q:
        in_specs += [pl.BlockSpec((LANES, tm), lambda b, i: (0, i)),
                     pl.BlockSpec((LANES, tm), lambda b, i: (0, i))]
        args += [cos.T, sin.T]
        in_specs += [pl.BlockSpec(w.shape, const2) for w in (w_dq, q_norm, w_qt, w_qrt)]
        args += [w_dq, q_norm, w_qt, w_qrt]
    in_specs += [pl.BlockSpec(w.shape, const2) for w in (w_kv, kv_norm, w_uk, w_uvt)]
    args += [w_kv, kv_norm, w_uk, w_uvt]
    out_specs = [pl.BlockSpec((1, tm, heads * HEAD_W), lambda b, i: (b, i, 0)),
                 pl.BlockSpec((1, heads * VT_ROWS, tm), lambda b, i: (b, 0, i))]
    out_shape = [jax.ShapeDtypeStruct((bsz, n, heads * HEAD_W), BF16),
                 jax.ShapeDtypeStruct((bsz, heads * VT_ROWS, n), BF16)]
    if with_q:
        out_specs = [pl.BlockSpec((1, heads * HEAD_W, tm), lambda b, i: (b, 0, i))] + out_specs
        out_shape = [jax.ShapeDtypeStruct((bsz, heads * HEAD_W, n), BF16)] + out_shape
    return pl.pallas_call(
        functools.partial(_mla_proj_kernel, heads=heads, q_lora=q_lora, kv_lora=kv_lora,
                          q_scale=q_scale, with_q=with_q),
        grid=(bsz, n // tm),
        in_specs=in_specs,
        out_specs=out_specs,
        out_shape=out_shape,
        compiler_params=_cparams(("parallel", "parallel")),
        name="mla_proj_q" if with_q else "mla_proj_kv",
    )(*args)


def _col_max(s):
    rows = s.shape[0]
    parts = [s[r * SUBLANES:(r + 1) * SUBLANES] for r in range(MAX_CHAINS)]
    for r in range(MAX_CHAINS, rows // SUBLANES):
        parts[r % MAX_CHAINS] = jnp.maximum(parts[r % MAX_CHAINS], s[r * SUBLANES:(r + 1) * SUBLANES])
    while len(parts) > 1:
        parts = [jnp.maximum(parts[i], parts[i + 1]) for i in range(0, len(parts), 2)]
    return jnp.max(parts[0], axis=0, keepdims=True)


def _flash_kernel(qt_ref, kc_ref, vct_ref, kx_ref, vxt_ref, o_ref, *scratch, tk, unroll):
    s_bufs, acc_ref = scratch[:RING], scratch[RING]
    qt = qt_ref[0]
    n_chunks = kx_ref.shape[1] // tk

    def pv(vt, p):
        return jnp.dot(vt, p, preferred_element_type=F32)

    def probs(s, m):
        return jnp.exp2((s - m).astype(BF16))

    s_ctx = jnp.dot(kc_ref[0], qt, preferred_element_type=F32)
    cmaxes = []
    for c in range(LOOKAHEAD):
        s = jnp.dot(kx_ref[0, c * tk:(c + 1) * tk, :], qt, preferred_element_type=F32)
        s_bufs[c][...] = s
        cmaxes.append(_col_max(s))
    m = _col_max(s_ctx)
    acc_ref[...] = pv(vct_ref[0], probs(s_ctx, m))

    def step(j, u, carry):
        m, cmax, *cmax_ahead = carry
        cur = pl.multiple_of(j * tk, tk)
        nxt = pl.multiple_of(jnp.minimum(j + LOOKAHEAD, n_chunks - 1) * tk, tk)
        s_next = jnp.dot(kx_ref[0, pl.ds(nxt, tk), :], qt, preferred_element_type=F32)
        s_bufs[(u + LOOKAHEAD) % RING][...] = s_next
        m_new = jnp.maximum(m, cmax)
        p = probs(s_bufs[u][...], m_new)
        acc_ref[...] = jnp.exp2(m - m_new) * acc_ref[...] + pv(vxt_ref[0, :, pl.ds(cur, tk)], p)
        return (m_new, *cmax_ahead, _col_max(s_next))

    def body(i, carry):
        for u in range(unroll):
            carry = step(unroll * i + u, u % RING, carry)
        return carry

    lax.fori_loop(0, n_chunks // unroll, body, (m, *cmaxes))
    acc = acc_ref[...]
    o_ref[0] = (acc[:V_DIM] / acc[V_DIM:V_DIM + 1]).astype(BF16)


def _flash(qt, kc, vct, kx, vxt, *, heads):
    bsz, _, n = qt.shape
    nc = kc.shape[1]
    tq = min(512, n)
    tk = min(512, n // RING)
    unroll = FLASH_UNROLL if (n // tk) % FLASH_UNROLL == 0 else RING
    assert (n // tk) % unroll == 0
    return pl.pallas_call(
        functools.partial(_flash_kernel, tk=tk, unroll=unroll),
        grid=(bsz, heads, n // tq),
        in_specs=[
            pl.BlockSpec((1, HEAD_W, tq), lambda b, h, i: (b, h, i)),
            pl.BlockSpec((1, nc, HEAD_W), lambda b, h, i: (b, 0, h)),
            pl.BlockSpec((1, VT_ROWS, nc), lambda b, h, i: (b, h, 0)),
            pl.BlockSpec((1, n, HEAD_W), lambda b, h, i: (b, 0, h)),
            pl.BlockSpec((1, VT_ROWS, n), lambda b, h, i: (b, h, 0)),
        ],
        out_specs=pl.BlockSpec((1, V_DIM, tq), lambda b, h, i: (b, h, i)),
        out_shape=jax.ShapeDtypeStruct((bsz, heads * V_DIM, n), BF16),
        scratch_shapes=[pltpu.VMEM((tk, tq), F32)] * RING + [pltpu.VMEM((VT_ROWS, tq), F32)],
        compiler_params=_cparams(("parallel", "parallel", "arbitrary")),
        name="flash",
    )(qt, kc, vct, kx, vxt)


def _attn_out_kernel(x_ref, at_ref, w_ref, mod_ref, o_ref):
    y = lax.dot_general(at_ref[0], w_ref[...], (((0,), (0,)), ((), ())),
                        preferred_element_type=F32)
    o_ref[0] = x_ref[0] + mod_ref[0, 5:6, :] * y


def _attn_out(x3, at, w, mod):
    bsz, n, d = x3.shape
    din = at.shape[1]
    tm = min(512, n)
    return pl.pallas_call(
        _attn_out_kernel,
        grid=(bsz, n // tm),
        in_specs=[
            pl.BlockSpec((1, tm, d), lambda b, i: (b, i, 0)),
            pl.BlockSpec((1, din, tm), lambda b, i: (b, 0, i)),
            pl.BlockSpec((din, d), lambda b, i: (0, 0)),
            pl.BlockSpec((1, N_MOD, d), lambda b, i: (b, 0, 0)),
        ],
        out_specs=pl.BlockSpec((1, tm, d), lambda b, i: (b, i, 0)),
        out_shape=jax.ShapeDtypeStruct((bsz, n, d), F32),
        compiler_params=_cparams(("parallel", "parallel")),
        name="attn_out",
    )(x3, at, w, mod)


def _rope_rot_cols(w):
    q = QK_ROPE // 4
    parts = w.reshape(*w.shape[:-1], 2, 2, q)
    rot = jnp.stack([-parts[..., 1, :], parts[..., 0, :]], axis=-2)
    return rot.reshape(w.shape)


def _pad_to(a, size, axis):
    pad = [(0, 0)] * a.ndim
    pad[axis] = (0, size - a.shape[axis])
    return jnp.pad(a, pad)


def _mla_weights(w_dq, q_norm, w_uq, w_dkv, kv_norm, w_ukv, heads):
    d, q_lora = w_dq.shape
    kv_lora = kv_norm.shape[0]
    qk_dim = QK_NOPE + QK_ROPE
    ql_pad = -(-q_lora // LANES) * LANES
    w_dq_p = _pad_to(w_dq, ql_pad, 1).astype(BF16)
    q_norm_p = _pad_to(q_norm, ql_pad, 0).reshape(1, ql_pad)
    uq = w_uq.reshape(q_lora, heads, qk_dim)
    uq_nope, uq_pe = uq[..., :QK_NOPE], uq[..., QK_NOPE:]
    zeros = jnp.zeros((q_lora, heads, LANES - QK_ROPE), w_uq.dtype)
    w_q = jnp.concatenate([uq_pe, zeros, uq_nope], axis=-1).reshape(q_lora, heads * HEAD_W)
    w_qr = jnp.concatenate([_rope_rot_cols(uq_pe), zeros], axis=-1).reshape(q_lora, heads * LANES)
    w_q = _pad_to(w_q, ql_pad, 0).astype(BF16)
    w_qr = _pad_to(w_qr, ql_pad, 0).astype(BF16)
    w_c = _pad_to(w_dkv[:, :kv_lora], 2 * LANES, 1)
    w_pe = w_dkv[:, kv_lora:]
    w_kv = jnp.concatenate([w_c, _pad_to(w_pe, LANES, 1), _pad_to(_rope_rot_cols(w_pe), LANES, 1)],
                           axis=1).astype(BF16)
    kv_norm_p = _pad_to(kv_norm, 2 * LANES, 0).reshape(1, 2 * LANES)
    ukv = w_ukv.reshape(kv_lora, heads, QK_NOPE + V_DIM)
    w_ukv_p = jnp.concatenate([ukv[..., :QK_NOPE].reshape(kv_lora, heads * QK_NOPE),
                               ukv[..., QK_NOPE:].reshape(kv_lora, heads * V_DIM)], axis=1)
    w_ukv_p = _pad_to(w_ukv_p, 2 * LANES, 0).astype(BF16)
    nope_w = heads * QK_NOPE
    w_uk, w_uvt = w_ukv_p[:, :nope_w], w_ukv_p[:, nope_w:].T
    return (w_dq_p, q_norm_p, w_q.T, w_qr.T, w_kv, kv_norm_p, w_uk, w_uvt), q_lora, kv_lora


def _rope_tables(n):
    t = jnp.arange(n)
    row = (t // GRID_W).astype(F32)
    col = (t % GRID_W).astype(F32)
    half = QK_ROPE // 4
    inv = ROPE_THETA ** (-jnp.arange(half, dtype=F32) / half)
    ang_r = row[:, None] * inv
    ang_c = col[:, None] * inv
    ang = jnp.concatenate([ang_r, ang_r, ang_c, ang_c], axis=-1)
    cos = _pad_to(jnp.cos(ang), LANES, 1)
    sin = _pad_to(jnp.sin(ang), LANES, 1)
    return cos, sin


def kernel(x, c, ctx, c_ctx, mod_w, mod_b, norm_g, ffn_w_gate, ffn_w_up, ffn_w_down,
           gm_w_in, gm_ln_g, gm_ln_b, gm_w_s, gm_b_s, gm_w_out,
           mla_w_dq, mla_q_norm, mla_w_uq, mla_w_dkv, mla_kv_norm, mla_w_ukv, mla_w_o,
           final_g):
    bsz, n, d = x.shape
    n_ctx = ctx.shape[1]
    depth = mod_w.shape[0]
    heads = d // V_DIM
    assert bsz + 1 <= 8

    cond8 = jnp.concatenate([c, c_ctx[None], jnp.zeros((8 - bsz - 1, d), F32)], axis=0)
    mods = _adaln(cond8, mod_w, mod_b).reshape(depth, 8, N_MOD, d)

    xt = x.reshape(bsz * n, d)
    yt = ctx.reshape(bsz * n_ctx, d)
    ffn_w = (ffn_w_gate.astype(BF16), ffn_w_up.astype(BF16), ffn_w_down.astype(BF16))

    for i in range(depth):
        last = i == depth - 1
        use_a = i % 2 == 0
        j = i // 2
        m_x = mods[i, :bsz]
        m_y = mods[i, bsz:bsz + 1]
        ctx_needed = (not last) or (not use_a)

        xt = _ffn(xt, m_x, 0, norm_g[i, 0], *ffn_w, i, 0)
        if ctx_needed:
            yt = _ffn(yt, m_y, 0, norm_g[i, 0], *ffn_w, i, 0)

        if use_a:
            inner = gm_w_out.shape[1]
            w_u = gm_w_in[j, :, :inner].astype(BF16)
            w_v = gm_w_in[j, :, inner:].astype(BF16)
            w_s = gm_w_s[j].astype(BF16)
            b_full = jnp.repeat(gm_b_s[j].T, inner // GM_GROUPS, axis=1)
            w_out = gm_w_out[j].astype(BF16)
            gm = (norm_g[i, 1], w_u, w_v, gm_ln_g[j], gm_ln_b[j], w_s, b_full)
            xt = _proj_res(xt, _gmlp_gate(xt, m_x, *gm), w_out, m_x)
            if not last:
                yt = _proj_res(yt, _gmlp_gate(yt, m_y, *gm), w_out, m_y)
        else:
            wts, q_lora, kv_lora = _mla_weights(mla_w_dq[j], mla_q_norm[j], mla_w_uq[j],
                                                mla_w_dkv[j], mla_kv_norm[j], mla_w_ukv[j], heads)
            q_scale = (QK_NOPE + QK_ROPE) ** -0.5 * math.log2(math.e)
            cos, sin = _rope_tables(n)
            ones = _pad_to(jnp.ones((n_ctx, QK_ROPE), F32), LANES, 1)
            zeros = jnp.zeros((n_ctx, LANES), F32)
            kw = dict(heads=heads, q_lora=q_lora, kv_lora=kv_lora, q_scale=q_scale)
            kc, vct = _mla_proj(yt.reshape(bsz, n_ctx, d), m_y, norm_g[i, 1], ones, zeros,
                                (None,) * 4 + wts[4:], with_q=False, **kw)
            qt, kx, vxt = _mla_proj(xt.reshape(bsz, n, d), m_x, norm_g[i, 1], cos, sin, wts,
                                    with_q=True, **kw)
            att_t = _flash(qt, kc, vct, kx, vxt, heads=heads)
            xt = _attn_out(xt.reshape(bsz, n, d), att_t, mla_w_o[j].astype(BF16), m_x)
            xt = xt.reshape(bsz * n, d)
            assert last, "context attention output is only needed for non-final MLA layers"

        xt = _ffn(xt, m_x, 2, norm_g[i, 2], *ffn_w, i, 1, final_g=final_g if last else None)
        if not last:
            yt = _ffn(yt, m_y, 2, norm_g[i, 2], *ffn_w, i, 1)

    return xt.reshape(bsz, n, d)
```
